```python
import math
import jax, jax.numpy as jnp
from jax import lax
import numpy as np

D_MODEL = 1024
BATCH = 32
SEQ = 2048
DEPTH = 1

PLE_DIM = 256
ATT_HEADS = 8
HEAD_DIM = 64
ATT_W = ATT_HEADS * HEAD_DIM
MOBA_BLOCK = 256
MOBA_TOPK = 3
Q_CHUNK = 128
CONV_CH = 512
CONV_WIDTH = 31
NUM_BUCKETS = 32
MAX_DISTANCE = 128
FFN_HIDDEN = -(-8 * D_MODEL // (3 * 256)) * 256
N_IN = 3 * ATT_W + 2 * CONV_CH + 2 * D_MODEL
IN_SPLITS = [ATT_W, 2 * ATT_W, 3 * ATT_W, 3 * ATT_W + 2 * CONV_CH]
DEEPNORM_ALPHA = (2.0 * DEPTH) ** 0.25
DEEPNORM_BETA = (8.0 * DEPTH) ** -0.25
LN_EPS = 1e-5
NEG_INF = -1e30

kernel_name = "hybrid_moba_conformer_deepnorm_block"


def _layer_norm(x, g, b):
    xf = x.astype(jnp.float32)
    mu = jnp.mean(xf, axis=-1, keepdims=True)
    var = jnp.mean(jnp.square(xf - mu), axis=-1, keepdims=True)
    y = (xf - mu) * lax.rsqrt(var + LN_EPS)
    return (y * g.astype(jnp.float32) + b.astype(jnp.float32)).astype(x.dtype)


def _t5_bucket(rel):
    n = jnp.maximum(rel, 0)
    max_exact = NUM_BUCKETS // 2
    nf = jnp.maximum(n, 1).astype(jnp.float32)
    large = max_exact + (jnp.log(nf / max_exact) / math.log(MAX_DISTANCE / max_exact)
                         * (NUM_BUCKETS - max_exact)).astype(jnp.int32)
    large = jnp.minimum(large, NUM_BUCKETS - 1)
    return jnp.where(n < max_exact, n, large)


def _moba_attention(q, k, v, bias_table):
    b, s, h, dh = q.shape
    nb = -(-s // MOBA_BLOCK)
    s_pad = nb * MOBA_BLOCK
    topk = min(MOBA_TOPK, nb)
    n_chunks = s // Q_CHUNK
    scale = HEAD_DIM ** -0.5
    q = q.transpose(0, 2, 1, 3)
    pad = ((0, 0), (0, 0), (0, s_pad - s), (0, 0))
    k = jnp.pad(k.transpose(0, 2, 1, 3), pad)
    v = jnp.pad(v.transpose(0, 2, 1, 3), pad)
    k_mean = jnp.mean(k.reshape(b, h, nb, MOBA_BLOCK, dh).astype(jnp.float32), axis=3)
    gate = jnp.einsum('bhsd,bhnd->bhsn', q.astype(jnp.float32), k_mean)
    q_blk = jnp.arange(s) // MOBA_BLOCK
    past = jnp.arange(nb)[None, :] < q_blk[:, None]
    gate = jnp.where(past, gate, NEG_INF)
    _, sel = lax.top_k(gate, topk)
    sel_valid = sel < q_blk[:, None]
    tbl = bias_table.T.astype(jnp.float32)
    head_ix = jnp.arange(h)

    def per_batch(args):
        q_b, k_b, v_b, sel_b, valid_b = args
        kb = k_b.reshape(h, nb, MOBA_BLOCK, dh)
        vb = v_b.reshape(h, nb, MOBA_BLOCK, dh)
        q_c = q_b.reshape(h, n_chunks, Q_CHUNK, dh).transpose(1, 0, 2, 3)
        sel_c = sel_b.reshape(h, n_chunks, Q_CHUNK, topk).transpose(1, 0, 2, 3)
        valid_c = valid_b.reshape(h, n_chunks, Q_CHUNK, topk).transpose(1, 0, 2, 3)

        def per_chunk(cargs):
            qc, selc, validc, c = cargs
            q_pos = c * Q_CHUNK + jnp.arange(Q_CHUNK)
            own = (c * Q_CHUNK) // MOBA_BLOCK
            k_own = lax.dynamic_index_in_dim(kb, own, axis=1, keepdims=False)
            v_own = lax.dynamic_index_in_dim(vb, own, axis=1, keepdims=False)
            rel_own = q_pos[:, None] - (own * MOBA_BLOCK + jnp.arange(MOBA_BLOCK))[None, :]
            logit_own = (jnp.einsum('hqd,hkd->hqk', qc, k_own).astype(jnp.float32) * scale
                         + tbl[:, _t5_bucket(rel_own)])
            logit_own = jnp.where(rel_own >= 0, logit_own, NEG_INF)
            k_sel = kb[head_ix[:, None, None], selc]
            v_sel = vb[head_ix[:, None, None], selc]
            k_pos_sel = selc[..., None] * MOBA_BLOCK + jnp.arange(MOBA_BLOCK)
            rel_sel = q_pos[None, :, None, None] - k_pos_sel
            bias_sel = tbl[head_ix[:, None, None, None], _t5_bucket(rel_sel)]
            logit_sel = (jnp.einsum('hqd,hqrkd->hqrk', qc, k_sel).astype(jnp.float32) * scale
                         + bias_sel)
            logit_sel = jnp.where(validc[..., None], logit_sel, NEG_INF)
            logits = jnp.concatenate(
                [logit_sel.reshape(h, Q_CHUNK, topk * MOBA_BLOCK), logit_own], axis=-1)
            probs = jax.nn.softmax(logits, axis=-1).astype(v_b.dtype)
            p_sel = probs[..., :topk * MOBA_BLOCK].reshape(h, Q_CHUNK, topk, MOBA_BLOCK)
            p_own = probs[..., topk * MOBA_BLOCK:]
            return (jnp.einsum('hqrk,hqrkd->hqd', p_sel, v_sel)
                    + jnp.einsum('hqk,hkd->hqd', p_own, v_own))

        out_c = lax.map(per_chunk, (q_c, sel_c, valid_c, jnp.arange(n_chunks)))
        return out_c.transpose(1, 0, 2, 3).reshape(h, s, dh)

    out = lax.map(per_batch, (q, k, v, sel, sel_valid))
    return out.transpose(0, 2, 1, 3).reshape(b, s, h * dh)


def _conformer_conv(u_in, conv_w, conv_b, ln_g, ln_b, w_out):
    a, g = jnp.split(u_in, 2, axis=-1)
    u = a * jax.nn.sigmoid(g)
    y = lax.conv_general_dilated(u, conv_w[:, None, :], window_strides=(1,),
                                 padding=[(CONV_WIDTH - 1, 0)],
                                 dimension_numbers=('NWC', 'WIO', 'NWC'),
                                 feature_group_count=CONV_CH) + conv_b
    y = jax.nn.silu(_layer_norm(y, ln_g, ln_b))
    return y @ w_out


def setup_inputs(seed: int = 0) -> dict:
    key = jax.random.key(seed)
    ks = jax.random.split(key, 24)
    f32 = jnp.float32
    nrm = lambda k, shape, s: jax.random.normal(k, shape, f32) * s
    return {
        "x": nrm(ks[0], (BATCH, SEQ, D_MODEL), 1.0),
        "p": nrm(ks[1], (DEPTH, BATCH, SEQ, PLE_DIM), 1.0),
        "w_in": nrm(ks[2], (DEPTH, D_MODEL, N_IN), D_MODEL ** -0.5),
        "b_gate": nrm(ks[3], (DEPTH, 2 * D_MODEL), 0.02),
        "bias_table": nrm(ks[4], (NUM_BUCKETS, ATT_HEADS), 0.1),
        "w_att_out": nrm(ks[5], (DEPTH, ATT_W, D_MODEL), ATT_W ** -0.5 * DEEPNORM_BETA),
        "conv_w": nrm(ks[6], (DEPTH, CONV_WIDTH, CONV_CH), CONV_WIDTH ** -0.5),
        "conv_b": nrm(ks[7], (DEPTH, CONV_CH), 0.02),
        "conv_ln_g": 1.0 + nrm(ks[8], (DEPTH, CONV_CH), 0.02),
        "conv_ln_b": nrm(ks[9], (DEPTH, CONV_CH), 0.02),
        "w_conv_out": nrm(ks[10], (DEPTH, CONV_CH, D_MODEL), CONV_CH ** -0.5 * DEEPNORM_BETA),
        "w_mix_out": nrm(ks[11], (DEPTH, D_MODEL, D_MODEL), D_MODEL ** -0.5 * DEEPNORM_BETA),
        "ln_mix_g": 1.0 + nrm(ks[12], (DEPTH, D_MODEL), 0.02),
        "ln_mix_b": nrm(ks[13], (DEPTH, D_MODEL), 0.02),
        "w_ffn_gate": nrm(ks[14], (DEPTH, D_MODEL, FFN_HIDDEN), D_MODEL ** -0.5),
        "w_ffn_up": nrm(ks[15], (DEPTH, D_MODEL, FFN_HIDDEN), D_MODEL ** -0.5),
        "w_ffn_down": nrm(ks[16], (DEPTH, FFN_HIDDEN, D_MODEL), FFN_HIDDEN ** -0.5 * DEEPNORM_BETA),
        "w_ple": nrm(ks[17], (DEPTH, PLE_DIM, D_MODEL), PLE_DIM ** -0.5 * DEEPNORM_BETA),
        "w_ple_gate": nrm(ks[18], (DEPTH, D_MODEL, D_MODEL), D_MODEL ** -0.5),
        "b_ple_gate": nrm(ks[19], (DEPTH, D_MODEL), 0.02),
        "ln_ffn_g": 1.0 + nrm(ks[20], (DEPTH, D_MODEL), 0.02),
        "ln_ffn_b": nrm(ks[21], (DEPTH, D_MODEL), 0.02),
    }


def reference(x, p, w_in, b_gate, bias_table, w_att_out, conv_w, conv_b, conv_ln_g, conv_ln_b,
              w_conv_out, w_mix_out, ln_mix_g, ln_mix_b, w_ffn_gate, w_ffn_up, w_ffn_down,
              w_ple, w_ple_gate, b_ple_gate, ln_ffn_g, ln_ffn_b):
    b, s, _ = x.shape
    for i in range(DEPTH):
        proj = x @ w_in[i]
        q, k, v, u_in, g_logits = jnp.split(proj, IN_SPLITS, axis=-1)
        q = q.reshape(b, s, ATT_HEADS, HEAD_DIM)
        k = k.reshape(b, s, ATT_HEADS, HEAD_DIM)
        v = v.reshape(b, s, ATT_HEADS, HEAD_DIM)
        y_att = _moba_attention(q, k, v, bias_table) @ w_att_out[i]
        y_conv = _conformer_conv(u_in, conv_w[i], conv_b[i], conv_ln_g[i], conv_ln_b[i],
                                 w_conv_out[i])
        g_att, g_conv = jnp.split(jax.nn.sigmoid(g_logits + b_gate[i]), 2, axis=-1)
        mixed = (g_att * y_att + g_conv * y_conv) @ w_mix_out[i]
        x = _layer_norm(DEEPNORM_ALPHA * x + mixed, ln_mix_g[i], ln_mix_b[i])
        hid = jax.nn.silu(x @ w_ffn_gate[i]) * (x @ w_ffn_up[i])
        ffn = hid @ w_ffn_down[i]
        ple = jax.nn.sigmoid(x @ w_ple_gate[i] + b_ple_gate[i]) * (p[i] @ w_ple[i])
        x = _layer_norm(DEEPNORM_ALPHA * x + ffn + ple, ln_ffn_g[i], ln_ffn_b[i])
    return x
```

```python
import functools
import math

import numpy as np
import jax
import jax.numpy as jnp
from jax import lax
from jax.experimental import pallas as pl
from jax.experimental.pallas import tpu as pltpu

D_MODEL = 1024
PLE_DIM = 256
ATT_HEADS = 8
HEAD_DIM = 64
ATT_W = ATT_HEADS * HEAD_DIM
MOBA_BLOCK = 256
MOBA_TOPK = 3
CONV_CH = 512
CONV_WIDTH = 31
NUM_BUCKETS = 32
MAX_DISTANCE = 128
FFN_HIDDEN = 2816
LN_EPS = 1e-5
NEG_INF = -1e30

LANES = 128
SUBLANES = 8
HEADS_PER_STEP = LANES // HEAD_DIM
CONV_PAD = 32
CONV_ROWS = 64
VMEM_LIMIT = 56 * 1024 * 1024

BF16 = jnp.bfloat16
F32 = jnp.float32
NT_DIMS = (((1,), (1,)), ((), ()))


def _resident(shape):
    return pl.BlockSpec(shape, lambda *_: (0,) * len(shape), pipeline_mode=pl.Buffered(1))


def _layer_norm(h, g, b):
    mu = jnp.mean(h, axis=-1, keepdims=True)
    d = h - mu
    var = jnp.mean(d * d, axis=-1, keepdims=True)
    return d * lax.rsqrt(var + LN_EPS) * g + b


def _t5_bucket_np(rel):
    n = np.maximum(rel, 0)
    max_exact = NUM_BUCKETS // 2
    nf = np.maximum(n, 1).astype(np.float32)
    large = max_exact + (np.log(nf / np.float32(max_exact)) / np.float32(math.log(MAX_DISTANCE / max_exact))
                         * np.float32(NUM_BUCKETS - max_exact)).astype(np.int32)
    large = np.minimum(large, NUM_BUCKETS - 1)
    return np.where(n < max_exact, n, large).astype(np.int32)


def _bias_bucket_tables():
    kpos = np.arange(MOBA_BLOCK)[:, None]
    qpos = np.arange(MOBA_BLOCK)[None, :]
    rel_own = qpos - kpos
    own = np.where(rel_own >= 0, _t5_bucket_np(rel_own), NUM_BUCKETS)
    prev = _t5_bucket_np(rel_own + MOBA_BLOCK)
    return own.astype(np.int32), prev.astype(np.int32)


def _bias_kernel(tbl_ref, own_ix_ref, prev_ix_ref, own_ref, prev_ref):
    h = pl.program_id(0)
    own_ix = own_ix_ref[...]
    prev_ix = prev_ix_ref[...]
    own = jnp.full(own_ix.shape, NEG_INF, F32)
    prev = jnp.zeros(prev_ix.shape, F32)
    for b in range(NUM_BUCKETS):
        t = tbl_ref[h, b]
        own = jnp.where(own_ix == b, t, own)
        prev = jnp.where(prev_ix == b, t, prev)
    own_ref[0] = own
    prev_ref[0] = prev


def _bias_tables(tbl_t):
    own_ix, prev_ix = _bias_bucket_tables()
    blk = (MOBA_BLOCK, MOBA_BLOCK)
    return pl.pallas_call(
        _bias_kernel,
        grid=(ATT_HEADS,),
        in_specs=[pl.BlockSpec(memory_space=pltpu.SMEM),
                  pl.BlockSpec(blk, lambda h: (0, 0)),
                  pl.BlockSpec(blk, lambda h: (0, 0))],
        out_specs=[pl.BlockSpec((1,) + blk, lambda h: (h, 0, 0)),
                   pl.BlockSpec((1,) + blk, lambda h: (h, 0, 0))],
        out_shape=[jax.ShapeDtypeStruct((ATT_HEADS,) + blk, F32)] * 2,
        name="t5_bias",
    )(tbl_t, jnp.asarray(own_ix), jnp.asarray(prev_ix))


def _proj_kernel(x_ref, wqk_ref, wvt_ref, wag_ref, wgate_ref, bgate_ref,
                 qk_ref, vt_ref, u_ref, gate_ref):
    xb = x_ref[0].astype(BF16)
    qk_ref[0] = jnp.dot(xb, wqk_ref[...], preferred_element_type=F32).astype(BF16)
    vt = lax.dot_general(wvt_ref[...], xb, NT_DIMS, preferred_element_type=F32)
    vt_ref[0] = vt.astype(BF16)
    ag = jnp.dot(xb, wag_ref[...], preferred_element_type=F32)
    u_ref[0] = (ag[:, :CONV_CH] * jax.nn.sigmoid(ag[:, CONV_CH:])).astype(BF16)
    gl = jnp.dot(xb, wgate_ref[...], preferred_element_type=F32) + bgate_ref[...]
    gate_ref[0] = jax.nn.sigmoid(gl).astype(BF16)


def _proj(x, wqk, wvt, wag, wgate, bgate, tm):
    b, s, d = x.shape
    return pl.pallas_call(
        _proj_kernel,
        grid=(b, s // tm),
        in_specs=[pl.BlockSpec((1, tm, d), lambda i, j: (i, j, 0)),
                  _resident(wqk.shape), _resident(wvt.shape), _resident(wag.shape),
                  _resident(wgate.shape), _resident(bgate.shape)],
        out_specs=[pl.BlockSpec((1, tm, 2 * ATT_W), lambda i, j: (i, j, 0)),
                   pl.BlockSpec((1, ATT_W, tm), lambda i, j: (i, 0, j)),
                   pl.BlockSpec((1, tm, CONV_CH), lambda i, j: (i, j, 0)),
                   pl.BlockSpec((1, tm, 2 * d), lambda i, j: (i, j, 0))],
        out_shape=[jax.ShapeDtypeStruct((b, s, 2 * ATT_W), BF16),
                   jax.ShapeDtypeStruct((b, ATT_W, s), BF16),
                   jax.ShapeDtypeStruct((b, s, CONV_CH), BF16),
                   jax.ShapeDtypeStruct((b, s, 2 * d), BF16)],
        compiler_params=pltpu.CompilerParams(
            dimension_semantics=("arbitrary", "arbitrary"), vmem_limit_bytes=VMEM_LIMIT),
        name="in_proj",
    )(x, wqk, wvt, wag, wgate, bgate)


def _attn_kernel(tbl_ref, q_ref, k_ref, vt_ref, bown_ref, bprev_ref, o_ref):
    hp = pl.program_id(1)
    s = q_ref.shape[1]
    nb = s // MOBA_BLOCK
    q = q_ref[0]
    k = k_ref[0]
    lane = lax.broadcasted_iota(jnp.int32, (1, LANES), 1)
    k_mean = (jnp.sum(k.astype(F32).reshape(nb, MOBA_BLOCK, LANES), axis=1)
              * (1.0 / MOBA_BLOCK)).astype(BF16)
    q_blk = lax.broadcasted_iota(jnp.int32, (1, s), 1) // MOBA_BLOCK
    blk = lax.broadcasted_iota(jnp.int32, (nb, 1), 0)
    past = blk < q_blk

    out_t = [[None] * HEADS_PER_STEP for _ in range(nb)]
    for hh in range(HEADS_PER_STEP):
        head = hp * HEADS_PER_STEP + hh
        qm = jnp.where(lane // HEAD_DIM == hh, q, jnp.zeros_like(q)) * BF16(HEAD_DIM ** -0.5)
        gate = lax.dot_general(k_mean, qm, NT_DIMS, preferred_element_type=F32)
        rank = jnp.zeros((nb, s), jnp.int32)
        for m in range(nb):
            gm = gate[m:m + 1, :]
            beats = (gm > gate) | ((gm == gate) & (m < blk))
            rank = rank + jnp.where(beats & (m < q_blk), 1, 0)
        sel_neg = jnp.where(past & (rank < MOBA_TOPK), 0.0, NEG_INF).astype(F32)
        far_bias = tbl_ref[head, NUM_BUCKETS - 1]
        for i in range(nb):
            lo, hi = i * MOBA_BLOCK, (i + 1) * MOBA_BLOCK
            st = lax.dot_general(k[:hi], qm[lo:hi], NT_DIMS, preferred_element_type=F32)
            pieces = []
            for j in range(i + 1):
                blk_t = st[j * MOBA_BLOCK:(j + 1) * MOBA_BLOCK]
                if j == i:
                    blk_t = blk_t + bown_ref[hh]
                elif j == i - 1:
                    blk_t = blk_t + bprev_ref[hh] + sel_neg[j:j + 1, lo:hi]
                else:
                    blk_t = blk_t + (sel_neg[j:j + 1, lo:hi] + far_bias)
                pieces.append(blk_t)
            st = jnp.concatenate(pieces, axis=0) if len(pieces) > 1 else pieces[0]
            mx = jnp.max(st, axis=0, keepdims=True)
            p = jnp.exp(st - mx)
            denom = jnp.sum(p, axis=0, keepdims=True)
            o_t = jnp.dot(vt_ref[0, hh * HEAD_DIM:(hh + 1) * HEAD_DIM, :hi], p.astype(BF16),
                          preferred_element_type=F32)
            out_t[i][hh] = o_t / denom
    for i in range(nb):
        o_pair = jnp.concatenate(out_t[i], axis=0)
        o_ref[0, i * MOBA_BLOCK:(i + 1) * MOBA_BLOCK, :] = o_pair.T.astype(o_ref.dtype)


def _attention(tbl_t, qk, vt, bias_own, bias_prev):
    b, s, _ = qk.shape
    n_groups = ATT_W // LANES
    blk = (HEADS_PER_STEP, MOBA_BLOCK, MOBA_BLOCK)
    return pl.pallas_call(
        _attn_kernel,
        grid=(b, n_groups),
        in_specs=[pl.BlockSpec(memory_space=pltpu.SMEM),
                  pl.BlockSpec((1, s, LANES), lambda i, g: (i, 0, g)),
                  pl.BlockSpec((1, s, LANES), lambda i, g: (i, 0, n_groups + g)),
                  pl.BlockSpec((1, LANES, s), lambda i, g: (i, g, 0)),
                  pl.BlockSpec(blk, lambda i, g: (g, 0, 0)),
                  pl.BlockSpec(blk, lambda i, g: (g, 0, 0))],
        out_specs=pl.BlockSpec((1, s, LANES), lambda i, g: (i, 0, g)),
        out_shape=jax.ShapeDtypeStruct((b, s, ATT_W), BF16),
        compiler_params=pltpu.CompilerParams(
            dimension_semantics=("arbitrary", "arbitrary"), vmem_limit_bytes=VMEM_LIMIT),
        name="moba_attn",
    )(tbl_t, qk, qk, vt, bias_own, bias_prev)


def _conv_kernel(u_ref, w_ref, b_ref, g_ref, beta_ref, o_ref, pad_ref):
    s = u_ref.shape[1]
    pad_ref[:CONV_PAD, :] = jnp.zeros((CONV_PAD, CONV_CH), F32)
    pad_ref[CONV_PAD:, :] = u_ref[0].astype(F32)
    first = CONV_PAD - (CONV_WIDTH - 1)

    def chunk(c, carry):
        base = pl.multiple_of(c * CONV_ROWS, CONV_ROWS)
        rows = CONV_ROWS + CONV_PAD
        window = pad_ref[pl.ds(base, rows), :]
        acc = jnp.zeros((CONV_ROWS, CONV_CH), F32)
        for r in range(SUBLANES):
            taps = [o - first for o in range(r, CONV_PAD + 1, SUBLANES) if 0 <= o - first < CONV_WIDTH]
            shifted = window if r == 0 else pltpu.roll(window, rows - r, 0)
            for j in taps:
                a8 = j + first - r
                acc = acc + shifted[a8:a8 + CONV_ROWS] * w_ref[j:j + 1, :]
        y = _layer_norm(acc + b_ref[...], g_ref[...], beta_ref[...])
        o_ref[0, pl.ds(base, CONV_ROWS), :] = (y * jax.nn.sigmoid(y)).astype(o_ref.dtype)
        return carry

    lax.fori_loop(0, s // CONV_ROWS, chunk, 0)


def _conv_module(u, conv_w, conv_b, ln_g, ln_b):
    b, s, c = u.shape
    return pl.pallas_call(
        _conv_kernel,
        grid=(b,),
        in_specs=[pl.BlockSpec((1, s, c), lambda i: (i, 0, 0)),
                  _resident(conv_w.shape), _resident(conv_b.shape),
                  _resident(ln_g.shape), _resident(ln_b.shape)],
        out_specs=pl.BlockSpec((1, s, c), lambda i: (i, 0, 0)),
        out_shape=jax.ShapeDtypeStruct((b, s, c), BF16),
        scratch_shapes=[pltpu.VMEM((CONV_PAD + s, c), F32)],
        compiler_params=pltpu.CompilerParams(
            dimension_semantics=("arbitrary",), vmem_limit_bytes=VMEM_LIMIT),
        name="conv_module",
    )(u, conv_w, conv_b, ln_g, ln_b)


def _mix_kernel(alpha, x_ref, att_ref, cv_ref, gate_ref, wa_ref, wc_ref, wm_ref, g_ref, b_ref, o_ref):
    d = x_ref.shape[1]
    y_att = jnp.dot(att_ref[...], wa_ref[...], preferred_element_type=F32)
    y_conv = jnp.dot(cv_ref[...], wc_ref[...], preferred_element_type=F32)
    merged = gate_ref[:, :d].astype(F32) * y_att + gate_ref[:, d:].astype(F32) * y_conv
    mixed = jnp.dot(merged.astype(BF16), wm_ref[...], preferred_element_type=F32)
    o_ref[...] = _layer_norm(alpha * x_ref[...] + mixed, g_ref[...], b_ref[...])


def _mix(alpha, x, att, cv, gates, wa, wc, wm, ln_g, ln_b, tm):
    t, d = x.shape
    row = lambda n: pl.BlockSpec((tm, n), lambda i: (i, 0))
    return pl.pallas_call(
        functools.partial(_mix_kernel, alpha),
        grid=(t // tm,),
        in_specs=[row(d), row(ATT_W), row(CONV_CH), row(2 * d),
                  _resident(wa.shape), _resident(wc.shape), _resident(wm.shape),
                  _resident(ln_g.shape), _resident(ln_b.shape)],
        out_specs=row(d),
        out_shape=jax.ShapeDtypeStruct((t, d), F32),
        compiler_params=pltpu.CompilerParams(
            dimension_semantics=("arbitrary",), vmem_limit_bytes=VMEM_LIMIT),
        name="mix_ln",
    )(x, att, cv, gates, wa, wc, wm, ln_g, ln_b)


def _ffn_kernel(alpha, x_ref, p_ref, wg_ref, wu_ref, wd_ref, wpg_ref, bpg_ref, wp_ref,
                g_ref, b_ref, o_ref):
    x = x_ref[...]
    xb = x.astype(BF16)
    hg = jnp.dot(xb, wg_ref[...], preferred_element_type=F32)
    hu = jnp.dot(xb, wu_ref[...], preferred_element_type=F32)
    hid = (hg * jax.nn.sigmoid(hg) * hu).astype(BF16)
    ffn = jnp.dot(hid, wd_ref[...], preferred_element_type=F32)
    pg = jax.nn.sigmoid(jnp.dot(xb, wpg_ref[...], preferred_element_type=F32) + bpg_ref[...])
    pe = jnp.dot(p_ref[...].astype(BF16), wp_ref[...], preferred_element_type=F32)
    o_ref[...] = _layer_norm(alpha * x + ffn + pg * pe, g_ref[...], b_ref[...])


def _ffn(alpha, x, p, wg, wu, wd, wpg, bpg, wp, ln_g, ln_b, tm):
    t, d = x.shape
    row = lambda n: pl.BlockSpec((tm, n), lambda i: (i, 0))
    return pl.pallas_call(
        functools.partial(_ffn_kernel, alpha),
        grid=(t // tm,),
        in_specs=[row(d), row(PLE_DIM),
                  _resident(wg.shape), _resident(wu.shape), _resident(wd.shape),
                  _resident(wpg.shape), _resident(bpg.shape), _resident(wp.shape),
                  _resident(ln_g.shape), _resident(ln_b.shape)],
        out_specs=row(d),
        out_shape=jax.ShapeDtypeStruct((t, d), F32),
        compiler_params=pltpu.CompilerParams(
            dimension_semantics=("arbitrary",), vmem_limit_bytes=VMEM_LIMIT),
        name="ffn_ple_ln",
    )(x, p, wg, wu, wd, wpg, bpg, wp, ln_g, ln_b)


def kernel(x, p, w_in, b_gate, bias_table, w_att_out, conv_w, conv_b, conv_ln_g, conv_ln_b,
           w_conv_out, w_mix_out, ln_mix_g, ln_mix_b, w_ffn_gate, w_ffn_up, w_ffn_down,
           w_ple, w_ple_gate, b_ple_gate, ln_ffn_g, ln_ffn_b):
    b, s, d = x.shape
    depth = w_in.shape[0]
    assert d == D_MODEL and s % MOBA_BLOCK == 0
    alpha = (2.0 * depth) ** 0.25
    tm = 512
    row = lambda v: v.reshape(1, -1).astype(F32)
    q_end, k_end, v_end = ATT_W, 2 * ATT_W, 3 * ATT_W
    u_end = v_end + 2 * CONV_CH

    tbl_t = bias_table.T.astype(F32)
    bias_own, bias_prev = _bias_tables(tbl_t)
    for i in range(depth):
        w = w_in[i].astype(BF16)
        qk, vt, u, gates = _proj(x, w[:, :k_end], w[:, k_end:v_end].T, w[:, v_end:u_end],
                                 w[:, u_end:], row(b_gate[i]), tm)
        att = _attention(tbl_t, qk, vt, bias_own, bias_prev)
        cv = _conv_module(u, conv_w[i].astype(F32), row(conv_b[i]), row(conv_ln_g[i]), row(conv_ln_b[i]))
        x1 = _mix(alpha, x.reshape(b * s, d), att.reshape(b * s, ATT_W), cv.reshape(b * s, CONV_CH),
                  gates.reshape(b * s, 2 * d), w_att_out[i].astype(BF16), w_conv_out[i].astype(BF16),
                  w_mix_out[i].astype(BF16), row(ln_mix_g[i]), row(ln_mix_b[i]), tm)
        x2 = _ffn(alpha, x1, p[i].reshape(b * s, PLE_DIM), w_ffn_gate[i].astype(BF16),
                  w_ffn_up[i].astype(BF16), w_ffn_down[i].astype(BF16), w_ple_gate[i].astype(BF16),
                  row(b_ple_gate[i]), w_ple[i].astype(BF16), row(ln_ffn_g[i]), row(ln_ffn_b[i]), tm)
        x = x2.reshape(b, s, d)
    return x
```

```python
import functools
import math

import numpy as np
import jax
import jax.numpy as jnp
from jax import lax
from jax.experimental import pallas as pl
from jax.experimental.pallas import tpu as pltpu

D_MODEL = 1024
PLE_DIM = 256
ATT_HEADS = 8
HEAD_DIM = 64
ATT_W = ATT_HEADS * HEAD_DIM
MOBA_BLOCK = 256
MOBA_TOPK = 3
CONV_CH = 512
CONV_WIDTH = 31
NUM_BUCKETS = 32
MAX_DISTANCE = 128
FFN_HIDDEN = 2816
LN_EPS = 1e-5
NEG_INF = -1e30

LANES = 128
SUBLANES = 8
HEADS_PER_STEP = LANES // HEAD_DIM
ONES_ROWS = 16
LOG2E = math.log2(math.e)
ATTN_SLOTS = 4
CONV_PAD = 32
CONV_ROWS = 64
VMEM_LIMIT = 56 * 1024 * 1024

BF16 = jnp.bfloat16
F32 = jnp.float32
NT_DIMS = (((1,), (1,)), ((), ()))


def _resident(shape):
    return pl.BlockSpec(shape, lambda *_: (0,) * len(shape), pipeline_mode=pl.Buffered(1))


def _layer_norm(h, g, b):
    mu = jnp.mean(h, axis=-1, keepdims=True)
    d = h - mu
    var = jnp.mean(d * d, axis=-1, keepdims=True)
    return d * lax.rsqrt(var + LN_EPS) * g + b


def _t5_bucket_np(rel):
    n = np.maximum(rel, 0)
    max_exact = NUM_BUCKETS // 2
    nf = np.maximum(n, 1).astype(np.float32)
    large = max_exact + (np.log(nf / np.float32(max_exact)) / np.float32(math.log(MAX_DISTANCE / max_exact))
                         * np.float32(NUM_BUCKETS - max_exact)).astype(np.int32)
    large = np.minimum(large, NUM_BUCKETS - 1)
    return np.where(n < max_exact, n, large).astype(np.int32)


def _bias_bucket_tables():
    kpos = np.arange(MOBA_BLOCK)[:, None]
    qpos = np.arange(MOBA_BLOCK)[None, :]
    rel_own = qpos - kpos
    own = np.where(rel_own >= 0, _t5_bucket_np(rel_own), NUM_BUCKETS)
    prev = _t5_bucket_np(rel_own + MOBA_BLOCK)
    return own.astype(np.int32), prev.astype(np.int32)


def _bias_kernel(tbl_ref, own_ix_ref, prev_ix_ref, own_ref, prev_ref):
    h = pl.program_id(0)
    own_ix = own_ix_ref[...]
    prev_ix = prev_ix_ref[...]
    own = jnp.full(own_ix.shape, NEG_INF, F32)
    prev = jnp.zeros(prev_ix.shape, F32)
    for b in range(NUM_BUCKETS):
        t = tbl_ref[h, b] * LOG2E
        own = jnp.where(own_ix == b, t, own)
        prev = jnp.where(prev_ix == b, t, prev)
    own_ref[0] = own
    prev_ref[0] = prev


def _bias_tables(tbl_t):
    own_ix, prev_ix = _bias_bucket_tables()
    blk = (MOBA_BLOCK, MOBA_BLOCK)
    return pl.pallas_call(
        _bias_kernel,
        grid=(ATT_HEADS,),
        in_specs=[pl.BlockSpec(memory_space=pltpu.SMEM),
                  pl.BlockSpec(blk, lambda h: (0, 0)),
                  pl.BlockSpec(blk, lambda h: (0, 0))],
        out_specs=[pl.BlockSpec((1,) + blk, lambda h: (h, 0, 0)),
                   pl.BlockSpec((1,) + blk, lambda h: (h, 0, 0))],
        out_shape=[jax.ShapeDtypeStruct((ATT_HEADS,) + blk, F32)] * 2,
        name="t5_bias",
    )(tbl_t, jnp.asarray(own_ix), jnp.asarray(prev_ix))


def _proj_kernel(x_ref, wqk_ref, wvt_ref, wag_ref, wgate_ref, bgate_ref,
                 qk_ref, vt_ref, u_ref, gate_ref):
    xb = x_ref[0].astype(BF16)
    qk = jnp.dot(xb, wqk_ref[...], preferred_element_type=F32)
    qk_ref[0, :, :ATT_W] = (qk[:, :ATT_W] * (LOG2E * HEAD_DIM ** -0.5)).astype(BF16)
    qk_ref[0, :, ATT_W:] = qk[:, ATT_W:].astype(BF16)
    vt = lax.dot_general(wvt_ref[...], xb, NT_DIMS, preferred_element_type=F32)
    vt_ref[0] = vt.astype(BF16)
    ag = jnp.dot(xb, wag_ref[...], preferred_element_type=F32)
    u_ref[0] = (ag[:, :CONV_CH] * jax.nn.sigmoid(ag[:, CONV_CH:])).astype(BF16)
    gl = jnp.dot(xb, wgate_ref[...], preferred_element_type=F32) + bgate_ref[...]
    gate_ref[0] = jax.nn.sigmoid(gl).astype(BF16)


def _proj(x, wqk, wvt, wag, wgate, bgate, tm):
    b, s, d = x.shape
    return pl.pallas_call(
        _proj_kernel,
        grid=(b, s // tm),
        in_specs=[pl.BlockSpec((1, tm, d), lambda i, j: (i, j, 0)),
                  _resident(wqk.shape), _resident(wvt.shape), _resident(wag.shape),
                  _resident(wgate.shape), _resident(bgate.shape)],
        out_specs=[pl.BlockSpec((1, tm, 2 * ATT_W), lambda i, j: (i, j, 0)),
                   pl.BlockSpec((1, ATT_W, tm), lambda i, j: (i, 0, j)),
                   pl.BlockSpec((1, tm, CONV_CH), lambda i, j: (i, j, 0)),
                   pl.BlockSpec((1, tm, 2 * d), lambda i, j: (i, j, 0))],
        out_shape=[jax.ShapeDtypeStruct((b, s, 2 * ATT_W), BF16),
                   jax.ShapeDtypeStruct((b, ATT_W, s), BF16),
                   jax.ShapeDtypeStruct((b, s, CONV_CH), BF16),
                   jax.ShapeDtypeStruct((b, s, 2 * d), BF16)],
        compiler_params=pltpu.CompilerParams(
            dimension_semantics=("arbitrary", "arbitrary"), vmem_limit_bytes=VMEM_LIMIT),
        name="in_proj",
    )(x, wqk, wvt, wag, wgate, bgate)


def _attn_kernel(tbl_ref, q_ref, k_ref, vt_ref, bown_ref, bprev_ref, o_ref,
                 qm_ref, vaug_ref, *sp_refs):
    s_refs, p_refs = sp_refs[:ATTN_SLOTS], sp_refs[ATTN_SLOTS:]
    hp = pl.program_id(1)
    s = q_ref.shape[1]
    nb = s // MOBA_BLOCK
    q = q_ref[0]
    k = k_ref[0]
    lane = lax.broadcasted_iota(jnp.int32, (1, LANES), 1)
    k_mean = (jnp.sum(k.astype(F32).reshape(nb, MOBA_BLOCK, LANES), axis=1)
              * (1.0 / MOBA_BLOCK)).astype(BF16)
    q_blk = lax.broadcasted_iota(jnp.int32, (1, s), 1) // MOBA_BLOCK
    blk = lax.broadcasted_iota(jnp.int32, (nb, 1), 0)
    past = blk < q_blk

    sel_neg, far_bias = [], []
    for hh in range(HEADS_PER_STEP):
        qm = jnp.where(lane // HEAD_DIM == hh, q, jnp.zeros_like(q))
        qm_ref[hh] = qm
        vaug_ref[hh, :HEAD_DIM, :] = vt_ref[0, hh * HEAD_DIM:(hh + 1) * HEAD_DIM, :]
        vaug_ref[hh, HEAD_DIM:, :] = jnp.ones((ONES_ROWS, s), BF16)
        gate = lax.dot_general(k_mean, qm, NT_DIMS, preferred_element_type=F32)
        rank = jnp.zeros((nb, s), jnp.int32)
        for m in range(nb):
            gm = gate[m:m + 1, :]
            beats = (gm > gate) | ((gm == gate) & (m < blk))
            rank = rank + jnp.where(beats & (m < q_blk), 1, 0)
        sel_neg.append(jnp.where(past & (rank < MOBA_TOPK), 0.0, NEG_INF).astype(F32))
        far_bias.append(tbl_ref[hp * HEADS_PER_STEP + hh, NUM_BUCKETS - 1] * LOG2E)

    rows = lambda j: slice(j * MOBA_BLOCK, (j + 1) * MOBA_BLOCK)
    shifts = {}

    slot_of = lambda i, hh: (i * HEADS_PER_STEP + hh) % ATTN_SLOTS

    def pass_a(i, hh):
        hi = (i + 1) * MOBA_BLOCK
        slot = slot_of(i, hh)
        st = lax.dot_general(k_ref[0, :hi, :], qm_ref[hh, rows(i), :], NT_DIMS,
                             preferred_element_type=F32)
        col_max, offs = None, []
        for j in range(i + 1):
            blk_t = st[rows(j)]
            if j == i:
                blk_t = blk_t + bown_ref[hh]
                off = None
            elif j == i - 1:
                blk_t = blk_t + bprev_ref[hh]
                off = sel_neg[hh][j:j + 1, rows(i)]
            else:
                off = sel_neg[hh][j:j + 1, rows(i)] + far_bias[hh]
            s_refs[slot][rows(j), :] = blk_t
            cand = jnp.max(blk_t, axis=0, keepdims=True)
            cand = cand if off is None else cand + off
            col_max = cand if col_max is None else jnp.maximum(col_max, cand)
            offs.append(off)
            yield
        shifts[i, hh] = [col_max if off is None else col_max - off for off in offs]

    def pass_b(i, hh):
        hi = (i + 1) * MOBA_BLOCK
        slot = slot_of(i, hh)
        for j in range(i + 1):
            p_refs[slot][rows(j), :] = jnp.exp2(s_refs[slot][rows(j), :] - shifts[i, hh][j]).astype(BF16)
            yield
        o_aug = jnp.dot(vaug_ref[hh, :, :hi], p_refs[slot][:hi, :], preferred_element_type=F32)
        outs[i, hh] = o_aug[:HEAD_DIM] / o_aug[HEAD_DIM:HEAD_DIM + 1]
        if hh == HEADS_PER_STEP - 1:
            o_pair = [outs.pop((i, h2)) for h2 in range(HEADS_PER_STEP)]
            o_ref[0, rows(i), :] = jnp.concatenate(o_pair, axis=0).T.astype(o_ref.dtype)

    n_chain = nb * HEADS_PER_STEP
    outs = {}
    for c in range(n_chain + ATTN_SLOTS - 1):
        d = c - (ATTN_SLOTS - 1)
        live = []
        if c < n_chain:
            live.append(pass_a(c // HEADS_PER_STEP, c % HEADS_PER_STEP))
        if d >= 0:
            live.append(pass_b(d // HEADS_PER_STEP, d % HEADS_PER_STEP))
        while live:
            live = [g for g in live if next(g, live) is not live]


def _attention(tbl_t, qk, vt, bias_own, bias_prev):
    b, s, _ = qk.shape
    n_groups = ATT_W // LANES
    blk = (HEADS_PER_STEP, MOBA_BLOCK, MOBA_BLOCK)
    return pl.pallas_call(
        _attn_kernel,
        grid=(b, n_groups),
        in_specs=[pl.BlockSpec(memory_space=pltpu.SMEM),
                  pl.BlockSpec((1, s, LANES), lambda i, g: (i, 0, g)),
                  pl.BlockSpec((1, s, LANES), lambda i, g: (i, 0, n_groups + g)),
                  pl.BlockSpec((1, LANES, s), lambda i, g: (i, g, 0)),
                  pl.BlockSpec(blk, lambda i, g: (g, 0, 0)),
                  pl.BlockSpec(blk, lambda i, g: (g, 0, 0))],
        out_specs=pl.BlockSpec((1, s, LANES), lambda i, g: (i, 0, g)),
        out_shape=jax.ShapeDtypeStruct((b, s, ATT_W), BF16),
        scratch_shapes=[pltpu.VMEM((HEADS_PER_STEP, s, LANES), BF16),
                        pltpu.VMEM((HEADS_PER_STEP, HEAD_DIM + ONES_ROWS, s), BF16),
                        *[pltpu.VMEM((s, MOBA_BLOCK), F32)] * ATTN_SLOTS,
                        *[pltpu.VMEM((s, MOBA_BLOCK), BF16)] * ATTN_SLOTS],
        compiler_params=pltpu.CompilerParams(
            dimension_semantics=("arbitrary", "arbitrary"), vmem_limit_bytes=VMEM_LIMIT),
        name="moba_attn",
    )(tbl_t, qk, qk, vt, bias_own, bias_prev)


def _conv_kernel(u_ref, w_ref, b_ref, g_ref, beta_ref, o_ref, pad_ref):
    s = u_ref.shape[1]
    pad_ref[:CONV_PAD, :] = jnp.zeros((CONV_PAD, CONV_CH), F32)
    pad_ref[CONV_PAD:, :] = u_ref[0].astype(F32)
    first = CONV_PAD - (CONV_WIDTH - 1)

    def chunk(c, carry):
        base = pl.multiple_of(c * CONV_ROWS, CONV_ROWS)
        rows = CONV_ROWS + CONV_PAD
        window = pad_ref[pl.ds(base, rows), :]
        acc = jnp.zeros((CONV_ROWS, CONV_CH), F32)
        for r in range(SUBLANES):
            taps = [o - first for o in range(r, CONV_PAD + 1, SUBLANES) if 0 <= o - first < CONV_WIDTH]
            shifted = window if r == 0 else pltpu.roll(window, rows - r, 0)
            for j in taps:
                a8 = j + first - r
                acc = acc + shifted[a8:a8 + CONV_ROWS] * w_ref[j:j + 1, :]
        y = _layer_norm(acc + b_ref[...], g_ref[...], beta_ref[...])
        o_ref[0, pl.ds(base, CONV_ROWS), :] = (y * jax.nn.sigmoid(y)).astype(o_ref.dtype)
        return carry

    lax.fori_loop(0, s // CONV_ROWS, chunk, 0)


def _conv_module(u, conv_w, conv_b, ln_g, ln_b):
    b, s, c = u.shape
    return pl.pallas_call(
        _conv_kernel,
        grid=(b,),
        in_specs=[pl.BlockSpec((1, s, c), lambda i: (i, 0, 0)),
                  _resident(conv_w.shape), _resident(conv_b.shape),
                  _resident(ln_g.shape), _resident(ln_b.shape)],
        out_specs=pl.BlockSpec((1, s, c), lambda i: (i, 0, 0)),
        out_shape=jax.ShapeDtypeStruct((b, s, c), BF16),
        scratch_shapes=[pltpu.VMEM((CONV_PAD + s, c), F32)],
        compiler_params=pltpu.CompilerParams(
            dimension_semantics=("arbitrary",), vmem_limit_bytes=VMEM_LIMIT),
        name="conv_module",
    )(u, conv_w, conv_b, ln_g, ln_b)


def _mix_kernel(alpha, x_ref, att_ref, cv_ref, gate_ref, wa_ref, wc_ref, wm_ref, g_ref, b_ref, o_ref):
    d = x_ref.shape[1]
    y_att = jnp.dot(att_ref[...], wa_ref[...], preferred_element_type=F32)
    y_conv = jnp.dot(cv_ref[...], wc_ref[...], preferred_element_type=F32)
    merged = gate_ref[:, :d].astype(F32) * y_att + gate_ref[:, d:].astype(F32) * y_conv
    mixed = jnp.dot(merged.astype(BF16), wm_ref[...], preferred_element_type=F32)
    o_ref[...] = _layer_norm(alpha * x_ref[...] + mixed, g_ref[...], b_ref[...])


def _mix(alpha, x, att, cv, gates, wa, wc, wm, ln_g, ln_b, tm):
    t, d = x.shape
    row = lambda n: pl.BlockSpec((tm, n), lambda i: (i, 0))
    return pl.pallas_call(
        functools.partial(_mix_kernel, alpha),
        grid=(t // tm,),
        in_specs=[row(d), row(ATT_W), row(CONV_CH), row(2 * d),
                  _resident(wa.shape), _resident(wc.shape), _resident(wm.shape),
                  _resident(ln_g.shape), _resident(ln_b.shape)],
        out_specs=row(d),
        out_shape=jax.ShapeDtypeStruct((t, d), F32),
        compiler_params=pltpu.CompilerParams(
            dimension_semantics=("arbitrary",), vmem_limit_bytes=VMEM_LIMIT),
        name="mix_ln",
    )(x, att, cv, gates, wa, wc, wm, ln_g, ln_b)


def _ffn_kernel(alpha, x_ref, p_ref, wg_ref, wu_ref, wd_ref, wpg_ref, bpg_ref, wp_ref,
                g_ref, b_ref, o_ref):
    x = x_ref[...]
    xb = x.astype(BF16)
    hg = jnp.dot(xb, wg_ref[...], preferred_element_type=F32)
    hu = jnp.dot(xb, wu_ref[...], preferred_element_type=F32)
    hid = (hg * jax.nn.sigmoid(hg) * hu).astype(BF16)
    ffn = jnp.dot(hid, wd_ref[...], preferred_element_type=F32)
    pg = jax.nn.sigmoid(jnp.dot(xb, wpg_ref[...], preferred_element_type=F32) + bpg_ref[...])
    pe = jnp.dot(p_ref[...].astype(BF16), wp_ref[...], preferred_element_type=F32)
    o_ref[...] = _layer_norm(alpha * x + ffn + pg * pe, g_ref[...], b_ref[...])


def _ffn(alpha, x, p, wg, wu, wd, wpg, bpg, wp, ln_g, ln_b, tm):
    t, d = x.shape
    row = lambda n: pl.BlockSpec((tm, n), lambda i: (i, 0))
    return pl.pallas_call(
        functools.partial(_ffn_kernel, alpha),
        grid=(t // tm,),
        in_specs=[row(d), row(PLE_DIM),
                  _resident(wg.shape), _resident(wu.shape), _resident(wd.shape),
                  _resident(wpg.shape), _resident(bpg.shape), _resident(wp.shape),
                  _resident(ln_g.shape), _resident(ln_b.shape)],
        out_specs=row(d),
        out_shape=jax.ShapeDtypeStruct((t, d), F32),
        compiler_params=pltpu.CompilerParams(
            dimension_semantics=("arbitrary",), vmem_limit_bytes=VMEM_LIMIT),
        name="ffn_ple_ln",
    )(x, p, wg, wu, wd, wpg, bpg, wp, ln_g, ln_b)


def kernel(x, p, w_in, b_gate, bias_table, w_att_out, conv_w, conv_b, conv_ln_g, conv_ln_b,
           w_conv_out, w_mix_out, ln_mix_g, ln_mix_b, w_ffn_gate, w_ffn_up, w_ffn_down,
           w_ple, w_ple_gate, b_ple_gate, ln_ffn_g, ln_ffn_b):
    b, s, d = x.shape
    depth = w_in.shape[0]
    assert d == D_MODEL and s % MOBA_BLOCK == 0
    alpha = (2.0 * depth) ** 0.25
    tm = 512
    row = lambda v: v.reshape(1, -1).astype(F32)
    q_end, k_end, v_end = ATT_W, 2 * ATT_W, 3 * ATT_W
    u_end = v_end + 2 * CONV_CH

    tbl_t = bias_table.T.astype(F32)
    bias_own, bias_prev = _bias_tables(tbl_t)
    for i in range(depth):
        w = w_in[i].astype(BF16)
        qk, vt, u, gates = _proj(x, w[:, :k_end], w[:, k_end:v_end].T, w[:, v_end:u_end],
                                 w[:, u_end:], row(b_gate[i]), tm)
        att = _attention(tbl_t, qk, vt, bias_own, bias_prev)
        cv = _conv_module(u, conv_w[i].astype(F32), row(conv_b[i]), row(conv_ln_g[i]), row(conv_ln_b[i]))
        x1 = _mix(alpha, x.reshape(b * s, d), att.reshape(b * s, ATT_W), cv.reshape(b * s, CONV_CH),
                  gates.reshape(b * s, 2 * d), w_att_out[i].astype(BF16), w_conv_out[i].astype(BF16),
                  w_mix_out[i].astype(BF16), row(ln_mix_g[i]), row(ln_mix_b[i]), tm)
        x2 = _ffn(alpha, x1, p[i].reshape(b * s, PLE_DIM), w_ffn_gate[i].astype(BF16),
                  w_ffn_up[i].astype(BF16), w_ffn_down[i].astype(BF16), w_ple_gate[i].astype(BF16),
                  row(b_ple_gate[i]), w_ple[i].astype(BF16), row(ln_ffn_g[i]), row(ln_ffn_b[i]), tm)
        x = x2.reshape(b, s, d)
    return x
```

```python
import functools
import math

import numpy as np
import jax
import jax.numpy as jnp
from jax import lax
from jax.experimental import pallas as pl
from jax.experimental.pallas import tpu as pltpu

D_MODEL = 1024
PLE_DIM = 256
ATT_HEADS = 8
HEAD_DIM = 64
ATT_W = ATT_HEADS * HEAD_DIM
MOBA_BLOCK = 256
MOBA_TOPK = 3
CONV_CH = 512
CONV_WIDTH = 31
NUM_BUCKETS = 32
MAX_DISTANCE = 128
FFN_HIDDEN = 2816
LN_EPS = 1e-5
NEG_INF = -1e30

LANES = 128
SUBLANES = 8
HEADS_PER_STEP = LANES // HEAD_DIM
ONES_ROWS = 16
LOG2E = math.log2(math.e)
ATTN_SLOTS = 4
CONV_PAD = 32
CONV_ROWS = 64
VMEM_LIMIT = 56 * 1024 * 1024

BF16 = jnp.bfloat16
F32 = jnp.float32
NT_DIMS = (((1,), (1,)), ((), ()))


def _resident(shape):
    return pl.BlockSpec(shape, lambda *_: (0,) * len(shape), pipeline_mode=pl.Buffered(1))


def _layer_norm(h, g, b):
    mu = jnp.mean(h, axis=-1, keepdims=True)
    d = h - mu
    var = jnp.mean(d * d, axis=-1, keepdims=True)
    return d * lax.rsqrt(var + LN_EPS) * g + b


def _t5_bucket_np(rel):
    n = np.maximum(rel, 0)
    max_exact = NUM_BUCKETS // 2
    nf = np.maximum(n, 1).astype(np.float32)
    large = max_exact + (np.log(nf / np.float32(max_exact)) / np.float32(math.log(MAX_DISTANCE / max_exact))
                         * np.float32(NUM_BUCKETS - max_exact)).astype(np.int32)
    large = np.minimum(large, NUM_BUCKETS - 1)
    return np.where(n < max_exact, n, large).astype(np.int32)


def _bias_bucket_tables():
    kpos = np.arange(MOBA_BLOCK)[:, None]
    qpos = np.arange(MOBA_BLOCK)[None, :]
    rel_own = qpos - kpos
    own = np.where(rel_own >= 0, _t5_bucket_np(rel_own), NUM_BUCKETS)
    prev = _t5_bucket_np(rel_own + MOBA_BLOCK)
    return own.astype(np.int32), prev.astype(np.int32)


def _bias_kernel(tbl_ref, own_ix_ref, prev_ix_ref, own_ref, prev_ref):
    h = pl.program_id(0)
    own_ix = own_ix_ref[...]
    prev_ix = prev_ix_ref[...]
    own = jnp.full(own_ix.shape, NEG_INF, F32)
    prev = jnp.zeros(prev_ix.shape, F32)
    for b in range(NUM_BUCKETS):
        t = tbl_ref[h, b] * LOG2E
        own = jnp.where(own_ix == b, t, own)
        prev = jnp.where(prev_ix == b, t, prev)
    own_ref[0] = own
    prev_ref[0] = prev


def _bias_tables(tbl_t):
    own_ix, prev_ix = _bias_bucket_tables()
    blk = (MOBA_BLOCK, MOBA_BLOCK)
    return pl.pallas_call(
        _bias_kernel,
        grid=(ATT_HEADS,),
        in_specs=[pl.BlockSpec(memory_space=pltpu.SMEM),
                  pl.BlockSpec(blk, lambda h: (0, 0)),
                  pl.BlockSpec(blk, lambda h: (0, 0))],
        out_specs=[pl.BlockSpec((1,) + blk, lambda h: (h, 0, 0)),
                   pl.BlockSpec((1,) + blk, lambda h: (h, 0, 0))],
        out_shape=[jax.ShapeDtypeStruct((ATT_HEADS,) + blk, F32)] * 2,
        name="t5_bias",
    )(tbl_t, jnp.asarray(own_ix), jnp.asarray(prev_ix))


def _proj_kernel(x_ref, wqk_ref, wvt_ref, wag_ref, wgate_ref, bgate_ref,
                 qk_ref, vt_ref, u_ref, gate_ref):
    xb = x_ref[0].astype(BF16)
    qk = jnp.dot(xb, wqk_ref[...], preferred_element_type=F32)
    qk_ref[0, :, :ATT_W] = (qk[:, :ATT_W] * (LOG2E * HEAD_DIM ** -0.5)).astype(BF16)
    qk_ref[0, :, ATT_W:] = qk[:, ATT_W:].astype(BF16)
    vt = lax.dot_general(wvt_ref[...], xb, NT_DIMS, preferred_element_type=F32)
    vt_ref[0] = vt.astype(BF16)
    ag = jnp.dot(xb, wag_ref[...], preferred_element_type=F32)
    u_ref[0] = (ag[:, :CONV_CH] * jax.nn.sigmoid(ag[:, CONV_CH:])).astype(BF16)
    gl = jnp.dot(xb, wgate_ref[...], preferred_element_type=F32) + bgate_ref[...]
    gate_ref[0] = jax.nn.sigmoid(gl).astype(BF16)


def _proj(x, wqk, wvt, wag, wgate, bgate, tm):
    b, s, d = x.shape
    return pl.pallas_call(
        _proj_kernel,
        grid=(b, s // tm),
        in_specs=[pl.BlockSpec((1, tm, d), lambda i, j: (i, j, 0)),
                  _resident(wqk.shape), _resident(wvt.shape), _resident(wag.shape),
                  _resident(wgate.shape), _resident(bgate.shape)],
        out_specs=[pl.BlockSpec((1, tm, 2 * ATT_W), lambda i, j: (i, j, 0)),
                   pl.BlockSpec((1, ATT_W, tm), lambda i, j: (i, 0, j)),
                   pl.BlockSpec((1, tm, CONV_CH), lambda i, j: (i, j, 0)),
                   pl.BlockSpec((1, tm, 2 * d), lambda i, j: (i, j, 0))],
        out_shape=[jax.ShapeDtypeStruct((b, s, 2 * ATT_W), BF16),
                   jax.ShapeDtypeStruct((b, ATT_W, s), BF16),
                   jax.ShapeDtypeStruct((b, s, CONV_CH), BF16),
                   jax.ShapeDtypeStruct((b, s, 2 * d), BF16)],
        compiler_params=pltpu.CompilerParams(
            dimension_semantics=("arbitrary", "arbitrary"), vmem_limit_bytes=VMEM_LIMIT),
        name="in_proj",
    )(x, wqk, wvt, wag, wgate, bgate)


def _attn_kernel(tbl_ref, q_ref, k_ref, vt_ref, bown_ref, bprev_ref, o_ref,
                 qm_ref, vaug_ref, *sp_refs):
    s_refs, p_refs = sp_refs[:ATTN_SLOTS], sp_refs[ATTN_SLOTS:]
    hp = pl.program_id(1)
    s = q_ref.shape[1]
    nb = s // MOBA_BLOCK
    q = q_ref[0]
    k = k_ref[0]
    lane = lax.broadcasted_iota(jnp.int32, (1, LANES), 1)
    k_mean = (jnp.sum(k.astype(F32).reshape(nb, MOBA_BLOCK, LANES), axis=1)
              * (1.0 / MOBA_BLOCK)).astype(BF16)
    q_blk = lax.broadcasted_iota(jnp.int32, (1, s), 1) // MOBA_BLOCK
    blk = lax.broadcasted_iota(jnp.int32, (nb, 1), 0)
    past = blk < q_blk

    sel_neg, far_bias = [], []
    for hh in range(HEADS_PER_STEP):
        qm = jnp.where(lane // HEAD_DIM == hh, q, jnp.zeros_like(q))
        qm_ref[hh] = qm
        vaug_ref[hh, :HEAD_DIM, :] = vt_ref[0, hh * HEAD_DIM:(hh + 1) * HEAD_DIM, :]
        vaug_ref[hh, HEAD_DIM:, :] = jnp.ones((ONES_ROWS, s), BF16)
        gate = lax.dot_general(k_mean, qm, NT_DIMS, preferred_element_type=F32)
        rank = jnp.zeros((nb, s), jnp.int32)
        for m in range(nb):
            gm = gate[m:m + 1, :]
            beats = (gm > gate) | ((gm == gate) & (m < blk))
            rank = rank + jnp.where(beats & (m < q_blk), 1, 0)
        sel_neg.append(jnp.where(past & (rank < MOBA_TOPK), 0.0, NEG_INF).astype(F32))
        far_bias.append(tbl_ref[hp * HEADS_PER_STEP + hh, NUM_BUCKETS - 1] * LOG2E)

    rows = lambda j: slice(j * MOBA_BLOCK, (j + 1) * MOBA_BLOCK)
    slot_of = lambda i, hh: (i * HEADS_PER_STEP + hh) % ATTN_SLOTS
    shifts = {}
    outs = {}

    def pass_a(i, hh):
        hi = (i + 1) * MOBA_BLOCK
        slot = slot_of(i, hh)
        st = lax.dot_general(k_ref[0, :hi, :], qm_ref[hh, rows(i), :], NT_DIMS,
                             preferred_element_type=F32)
        col_max, offs = None, []
        for j in range(i + 1):
            blk_t = st[rows(j)]
            if j == i:
                blk_t = blk_t + bown_ref[hh]
                off = None
            elif j == i - 1:
                blk_t = blk_t + bprev_ref[hh]
                off = sel_neg[hh][j:j + 1, rows(i)]
            else:
                off = sel_neg[hh][j:j + 1, rows(i)] + far_bias[hh]
            s_refs[slot][rows(j), :] = blk_t
            cand = jnp.max(blk_t, axis=0, keepdims=True)
            cand = cand if off is None else cand + off
            col_max = cand if col_max is None else jnp.maximum(col_max, cand)
            offs.append(off)
        shifts[i, hh] = [col_max if off is None else col_max - off for off in offs]

    def pass_b(i, hh):
        hi = (i + 1) * MOBA_BLOCK
        slot = slot_of(i, hh)
        for j in range(i + 1):
            p_refs[slot][rows(j), :] = jnp.exp2(s_refs[slot][rows(j), :] - shifts[i, hh][j]).astype(BF16)
        o_aug = jnp.dot(vaug_ref[hh, :, :hi], p_refs[slot][:hi, :], preferred_element_type=F32)
        outs[i, hh] = o_aug[:HEAD_DIM] / o_aug[HEAD_DIM:HEAD_DIM + 1]
        if hh == HEADS_PER_STEP - 1:
            o_pair = [outs.pop((i, h2)) for h2 in range(HEADS_PER_STEP)]
            o_ref[0, rows(i), :] = jnp.concatenate(o_pair, axis=0).T.astype(o_ref.dtype)

    n_chain = nb * HEADS_PER_STEP
    for c in range(n_chain + ATTN_SLOTS - 1):
        if c < n_chain:
            pass_a(c // HEADS_PER_STEP, c % HEADS_PER_STEP)
        d = c - (ATTN_SLOTS - 1)
        if d >= 0:
            pass_b(d // HEADS_PER_STEP, d % HEADS_PER_STEP)


def _attention(tbl_t, qk, vt, bias_own, bias_prev):
    b, s, _ = qk.shape
    n_groups = ATT_W // LANES
    blk = (HEADS_PER_STEP, MOBA_BLOCK, MOBA_BLOCK)
    return pl.pallas_call(
        _attn_kernel,
        grid=(b, n_groups),
        in_specs=[pl.BlockSpec(memory_space=pltpu.SMEM),
                  pl.BlockSpec((1, s, LANES), lambda i, g: (i, 0, g)),
                  pl.BlockSpec((1, s, LANES), lambda i, g: (i, 0, n_groups + g)),
                  pl.BlockSpec((1, LANES, s), lambda i, g: (i, g, 0)),
                  pl.BlockSpec(blk, lambda i, g: (g, 0, 0)),
                  pl.BlockSpec(blk, lambda i, g: (g, 0, 0))],
        out_specs=pl.BlockSpec((1, s, LANES), lambda i, g: (i, 0, g)),
        out_shape=jax.ShapeDtypeStruct((b, s, ATT_W), BF16),
        scratch_shapes=[pltpu.VMEM((HEADS_PER_STEP, s, LANES), BF16),
                        pltpu.VMEM((HEADS_PER_STEP, HEAD_DIM + ONES_ROWS, s), BF16),
                        *[pltpu.VMEM((s, MOBA_BLOCK), F32)] * ATTN_SLOTS,
                        *[pltpu.VMEM((s, MOBA_BLOCK), BF16)] * ATTN_SLOTS],
        compiler_params=pltpu.CompilerParams(
            dimension_semantics=("arbitrary", "arbitrary"), vmem_limit_bytes=VMEM_LIMIT),
        name="moba_attn",
    )(tbl_t, qk, qk, vt, bias_own, bias_prev)


def _conv_kernel(u_ref, w_ref, b_ref, g_ref, beta_ref, o_ref, pad_ref):
    s = u_ref.shape[1]
    pad_ref[:CONV_PAD, :] = jnp.zeros((CONV_PAD, CONV_CH), F32)
    pad_ref[CONV_PAD:, :] = u_ref[0].astype(F32)
    first = CONV_PAD - (CONV_WIDTH - 1)

    def chunk(c, carry):
        base = pl.multiple_of(c * CONV_ROWS, CONV_ROWS)
        rows = CONV_ROWS + CONV_PAD
        window = pad_ref[pl.ds(base, rows), :]
        acc = jnp.zeros((CONV_ROWS, CONV_CH), F32)
        for r in range(SUBLANES):
            taps = [o - first for o in range(r, CONV_PAD + 1, SUBLANES) if 0 <= o - first < CONV_WIDTH]
            shifted = window if r == 0 else pltpu.roll(window, rows - r, 0)
            for j in taps:
                a8 = j + first - r
                acc = acc + shifted[a8:a8 + CONV_ROWS] * w_ref[j:j + 1, :]
        y = _layer_norm(acc + b_ref[...], g_ref[...], beta_ref[...])
        o_ref[0, pl.ds(base, CONV_ROWS), :] = (y * jax.nn.sigmoid(y)).astype(o_ref.dtype)
        return carry

    lax.fori_loop(0, s // CONV_ROWS, chunk, 0)


def _conv_module(u, conv_w, conv_b, ln_g, ln_b):
    b, s, c = u.shape
    return pl.pallas_call(
        _conv_kernel,
        grid=(b,),
        in_specs=[pl.BlockSpec((1, s, c), lambda i: (i, 0, 0)),
                  _resident(conv_w.shape), _resident(conv_b.shape),
                  _resident(ln_g.shape), _resident(ln_b.shape)],
        out_specs=pl.BlockSpec((1, s, c), lambda i: (i, 0, 0)),
        out_shape=jax.ShapeDtypeStruct((b, s, c), BF16),
        scratch_shapes=[pltpu.VMEM((CONV_PAD + s, c), F32)],
        compiler_params=pltpu.CompilerParams(
            dimension_semantics=("arbitrary",), vmem_limit_bytes=VMEM_LIMIT),
        name="conv_module",
    )(u, conv_w, conv_b, ln_g, ln_b)


def _mix_ffn_kernel(alpha, x_ref, att_ref, cv_ref, gate_ref, p_ref,
                    wa_ref, wc_ref, wm_ref, g1_ref, b1_ref,
                    wg_ref, wu_ref, wd_ref, wpg_ref, bpg_ref, wp_ref, g2_ref, b2_ref, o_ref):
    d = x_ref.shape[1]
    y_att = jnp.dot(att_ref[...], wa_ref[...], preferred_element_type=F32)
    y_conv = jnp.dot(cv_ref[...], wc_ref[...], preferred_element_type=F32)
    merged = gate_ref[:, :d].astype(F32) * y_att + gate_ref[:, d:].astype(F32) * y_conv
    mixed = jnp.dot(merged.astype(BF16), wm_ref[...], preferred_element_type=F32)
    x1 = _layer_norm(alpha * x_ref[...] + mixed, g1_ref[...], b1_ref[...])

    xb = x1.astype(BF16)
    hg = jnp.dot(xb, wg_ref[...], preferred_element_type=F32)
    hu = jnp.dot(xb, wu_ref[...], preferred_element_type=F32)
    hid = (hg * jax.nn.sigmoid(hg) * hu).astype(BF16)
    ffn = jnp.dot(hid, wd_ref[...], preferred_element_type=F32)
    pg = jax.nn.sigmoid(jnp.dot(xb, wpg_ref[...], preferred_element_type=F32) + bpg_ref[...])
    pe = jnp.dot(p_ref[...].astype(BF16), wp_ref[...], preferred_element_type=F32)
    o_ref[...] = _layer_norm(alpha * x1 + ffn + pg * pe, g2_ref[...], b2_ref[...])


def _mix_ffn(alpha, x, att, cv, gates, p, weights, tm):
    t, d = x.shape
    row = lambda n: pl.BlockSpec((tm, n), lambda i: (i, 0))
    return pl.pallas_call(
        functools.partial(_mix_ffn_kernel, alpha),
        grid=(t // tm,),
        in_specs=[row(d), row(ATT_W), row(CONV_CH), row(2 * d), row(PLE_DIM)]
                 + [_resident(w.shape) for w in weights],
        out_specs=row(d),
        out_shape=jax.ShapeDtypeStruct((t, d), F32),
        compiler_params=pltpu.CompilerParams(
            dimension_semantics=("arbitrary",), vmem_limit_bytes=VMEM_LIMIT),
        name="mix_ffn",
    )(x, att, cv, gates, p, *weights)


def kernel(x, p, w_in, b_gate, bias_table, w_att_out, conv_w, conv_b, conv_ln_g, conv_ln_b,
           w_conv_out, w_mix_out, ln_mix_g, ln_mix_b, w_ffn_gate, w_ffn_up, w_ffn_down,
           w_ple, w_ple_gate, b_ple_gate, ln_ffn_g, ln_ffn_b):
    b, s, d = x.shape
    depth = w_in.shape[0]
    assert d == D_MODEL and s % MOBA_BLOCK == 0
    alpha = (2.0 * depth) ** 0.25
    tm = 512
    tm_ffn = 512
    row = lambda v: v.reshape(1, -1).astype(F32)
    k_end, v_end = 2 * ATT_W, 3 * ATT_W
    u_end = v_end + 2 * CONV_CH

    tbl_t = bias_table.T.astype(F32)
    bias_own, bias_prev = _bias_tables(tbl_t)
    for i in range(depth):
        w = w_in[i].astype(BF16)
        qk, vt, u, gates = _proj(x, w[:, :k_end], w[:, k_end:v_end].T, w[:, v_end:u_end],
                                 w[:, u_end:], row(b_gate[i]), tm)
        att = _attention(tbl_t, qk, vt, bias_own, bias_prev)
        cv = _conv_module(u, conv_w[i].astype(F32), row(conv_b[i]), row(conv_ln_g[i]), row(conv_ln_b[i]))
        weights = (w_att_out[i].astype(BF16), w_conv_out[i].astype(BF16), w_mix_out[i].astype(BF16),
                   row(ln_mix_g[i]), row(ln_mix_b[i]),
                   w_ffn_gate[i].astype(BF16), w_ffn_up[i].astype(BF16), w_ffn_down[i].astype(BF16),
                   w_ple_gate[i].astype(BF16), row(b_ple_gate[i]), w_ple[i].astype(BF16),
                   row(ln_ffn_g[i]), row(ln_ffn_b[i]))
        x2 = _mix_ffn(alpha, x.reshape(b * s, d), att.reshape(b * s, ATT_W), cv.reshape(b * s, CONV_CH),
                      gates.reshape(b * s, 2 * d), p[i].reshape(b * s, PLE_DIM), weights, tm_ffn)
        x = x2.reshape(b, s, d)
    return x
```

```python
import functools
import math

import numpy as np
import jax
import jax.numpy as jnp
from jax import lax
from jax.experimental import pallas as pl
from jax.experimental.pallas import tpu as pltpu

D_MODEL = 1024
PLE_DIM = 256
ATT_HEADS = 8
HEAD_DIM = 64
ATT_W = ATT_HEADS * HEAD_DIM
MOBA_BLOCK = 256
MOBA_TOPK = 3
CONV_CH = 512
CONV_WIDTH = 31
NUM_BUCKETS = 32
MAX_DISTANCE = 128
FFN_HIDDEN = 2816
LN_EPS = 1e-5
NEG_INF = -1e30

LANES = 128
SUBLANES = 8
HEADS_PER_STEP = LANES // HEAD_DIM
ONES_ROWS = 16
LOG2E = math.log2(math.e)
ATTN_SLOTS = 4
CONV_BLK = LANES
CONV_ROW_TILE = 64
CONV_CH_TILE = 64
CONV_ROW_PITCH = CONV_CH + 8
VMEM_LIMIT = 56 * 1024 * 1024

BF16 = jnp.bfloat16
F32 = jnp.float32
NT_DIMS = (((1,), (1,)), ((), ()))


def _resident(shape):
    return pl.BlockSpec(shape, lambda *_: (0,) * len(shape), pipeline_mode=pl.Buffered(1))


def _layer_norm(h, g, b):
    mu = jnp.mean(h, axis=-1, keepdims=True)
    d = h - mu
    var = jnp.mean(d * d, axis=-1, keepdims=True)
    return d * lax.rsqrt(var + LN_EPS) * g + b


def _t5_bucket_np(rel):
    n = np.maximum(rel, 0)
    max_exact = NUM_BUCKETS // 2
    nf = np.maximum(n, 1).astype(np.float32)
    large = max_exact + (np.log(nf / np.float32(max_exact)) / np.float32(math.log(MAX_DISTANCE / max_exact))
                         * np.float32(NUM_BUCKETS - max_exact)).astype(np.int32)
    large = np.minimum(large, NUM_BUCKETS - 1)
    return np.where(n < max_exact, n, large).astype(np.int32)


def _bias_bucket_tables():
    kpos = np.arange(MOBA_BLOCK)[:, None]
    qpos = np.arange(MOBA_BLOCK)[None, :]
    rel_own = qpos - kpos
    own = np.where(rel_own >= 0, _t5_bucket_np(rel_own), NUM_BUCKETS)
    prev = _t5_bucket_np(rel_own + MOBA_BLOCK)
    return own.astype(np.int32), prev.astype(np.int32)


def _bias_kernel(tbl_ref, own_ix_ref, prev_ix_ref, own_ref, prev_ref):
    h = pl.program_id(0)
    own_ix = own_ix_ref[...]
    prev_ix = prev_ix_ref[...]
    own = jnp.full(own_ix.shape, NEG_INF, F32)
    prev = jnp.zeros(prev_ix.shape, F32)
    for b in range(NUM_BUCKETS):
        t = tbl_ref[h, b] * LOG2E
        own = jnp.where(own_ix == b, t, own)
        prev = jnp.where(prev_ix == b, t, prev)
    own_ref[0] = own
    prev_ref[0] = prev


def _bias_tables(tbl_t):
    own_ix, prev_ix = _bias_bucket_tables()
    blk = (MOBA_BLOCK, MOBA_BLOCK)
    return pl.pallas_call(
        _bias_kernel,
        grid=(ATT_HEADS,),
        in_specs=[pl.BlockSpec(memory_space=pltpu.SMEM),
                  pl.BlockSpec(blk, lambda h: (0, 0)),
                  pl.BlockSpec(blk, lambda h: (0, 0))],
        out_specs=[pl.BlockSpec((1,) + blk, lambda h: (h, 0, 0)),
                   pl.BlockSpec((1,) + blk, lambda h: (h, 0, 0))],
        out_shape=[jax.ShapeDtypeStruct((ATT_HEADS,) + blk, F32)] * 2,
        name="t5_bias",
    )(tbl_t, jnp.asarray(own_ix), jnp.asarray(prev_ix))


def _proj_kernel(x_ref, wqk_ref, wvt_ref, wagt_ref, wgate_ref, bgate_ref,
                 qk_ref, vt_ref, ut_ref, gate_ref):
    xb = x_ref[0].astype(BF16)
    qk = jnp.dot(xb, wqk_ref[...], preferred_element_type=F32)
    qk_ref[0, :, :ATT_W] = (qk[:, :ATT_W] * (LOG2E * HEAD_DIM ** -0.5)).astype(BF16)
    qk_ref[0, :, ATT_W:] = qk[:, ATT_W:].astype(BF16)
    vt = lax.dot_general(wvt_ref[...], xb, NT_DIMS, preferred_element_type=F32)
    vt_ref[0] = vt.astype(BF16)
    agt = lax.dot_general(wagt_ref[...], xb, NT_DIMS, preferred_element_type=F32)
    ut_ref[0] = (agt[:CONV_CH] * jax.nn.sigmoid(agt[CONV_CH:])).astype(BF16)
    gl = jnp.dot(xb, wgate_ref[...], preferred_element_type=F32) + bgate_ref[...]
    gate_ref[0] = jax.nn.sigmoid(gl).astype(BF16)


def _proj(x, wqk, wvt, wagt, wgate, bgate, tm):
    b, s, d = x.shape
    return pl.pallas_call(
        _proj_kernel,
        grid=(b, s // tm),
        in_specs=[pl.BlockSpec((1, tm, d), lambda i, j: (i, j, 0)),
                  _resident(wqk.shape), _resident(wvt.shape), _resident(wagt.shape),
                  _resident(wgate.shape), _resident(bgate.shape)],
        out_specs=[pl.BlockSpec((1, tm, 2 * ATT_W), lambda i, j: (i, j, 0)),
                   pl.BlockSpec((1, ATT_W, tm), lambda i, j: (i, 0, j)),
                   pl.BlockSpec((1, CONV_CH, tm), lambda i, j: (i, 0, j)),
                   pl.BlockSpec((1, tm, 2 * d), lambda i, j: (i, j, 0))],
        out_shape=[jax.ShapeDtypeStruct((b, s, 2 * ATT_W), BF16),
                   jax.ShapeDtypeStruct((b, ATT_W, s), BF16),
                   jax.ShapeDtypeStruct((b, CONV_CH, s), BF16),
                   jax.ShapeDtypeStruct((b, s, 2 * d), BF16)],
        compiler_params=pltpu.CompilerParams(
            dimension_semantics=("arbitrary", "arbitrary"), vmem_limit_bytes=VMEM_LIMIT),
        name="in_proj",
    )(x, wqk, wvt, wagt, wgate, bgate)


def _attn_kernel(tbl_ref, q_ref, k_ref, vt_ref, bown_ref, bprev_ref, o_ref,
                 qm_ref, vaug_ref, *sp_refs):
    s_refs, p_refs = sp_refs[:ATTN_SLOTS], sp_refs[ATTN_SLOTS:]
    hp = pl.program_id(1)
    s = q_ref.shape[1]
    nb = s // MOBA_BLOCK
    q = q_ref[0]
    k = k_ref[0]
    lane = lax.broadcasted_iota(jnp.int32, (1, LANES), 1)
    k_mean = (jnp.sum(k.astype(F32).reshape(nb, MOBA_BLOCK, LANES), axis=1)
              * (1.0 / MOBA_BLOCK)).astype(BF16)
    q_blk = lax.broadcasted_iota(jnp.int32, (1, s), 1) // MOBA_BLOCK
    blk = lax.broadcasted_iota(jnp.int32, (nb, 1), 0)
    past = blk < q_blk

    sel_neg, far_bias = [], []
    for hh in range(HEADS_PER_STEP):
        qm = jnp.where(lane // HEAD_DIM == hh, q, jnp.zeros_like(q))
        qm_ref[hh] = qm
        vaug_ref[hh, :HEAD_DIM, :] = vt_ref[0, hh * HEAD_DIM:(hh + 1) * HEAD_DIM, :]
        vaug_ref[hh, HEAD_DIM:, :] = jnp.ones((ONES_ROWS, s), BF16)
        gate = lax.dot_general(k_mean, qm, NT_DIMS, preferred_element_type=F32)
        rank = jnp.zeros((nb, s), jnp.int32)
        for m in range(nb):
            gm = gate[m:m + 1, :]
            beats = (gm > gate) | ((gm == gate) & (m < blk))
            rank = rank + jnp.where(beats & (m < q_blk), 1, 0)
        sel_neg.append(jnp.where(past & (rank < MOBA_TOPK), 0.0, NEG_INF).astype(F32))
        far_bias.append(tbl_ref[hp * HEADS_PER_STEP + hh, NUM_BUCKETS - 1] * LOG2E)

    rows = lambda j: slice(j * MOBA_BLOCK, (j + 1) * MOBA_BLOCK)
    slot_of = lambda i, hh: (i * HEADS_PER_STEP + hh) % ATTN_SLOTS
    shifts = {}
    outs = {}

    def pass_a(i, hh):
        hi = (i + 1) * MOBA_BLOCK
        slot = slot_of(i, hh)
        st = lax.dot_general(k_ref[0, :hi, :], qm_ref[hh, rows(i), :], NT_DIMS,
                             preferred_element_type=F32)
        col_max, offs = None, []
        for j in range(i + 1):
            blk_t = st[rows(j)]
            if j == i:
                blk_t = blk_t + bown_ref[hh]
                off = None
            elif j == i - 1:
                blk_t = blk_t + bprev_ref[hh]
                off = sel_neg[hh][j:j + 1, rows(i)]
            else:
                off = sel_neg[hh][j:j + 1, rows(i)] + far_bias[hh]
            s_refs[slot][rows(j), :] = blk_t
            cand = jnp.max(blk_t, axis=0, keepdims=True)
            cand = cand if off is None else cand + off
            col_max = cand if col_max is None else jnp.maximum(col_max, cand)
            offs.append(off)
        shifts[i, hh] = [col_max if off is None else col_max - off for off in offs]

    def pass_b(i, hh):
        hi = (i + 1) * MOBA_BLOCK
        slot = slot_of(i, hh)
        for j in range(i + 1):
            p_refs[slot][rows(j), :] = jnp.exp2(s_refs[slot][rows(j), :] - shifts[i, hh][j]).astype(BF16)
        o_aug = jnp.dot(vaug_ref[hh, :, :hi], p_refs[slot][:hi, :], preferred_element_type=F32)
        outs[i, hh] = o_aug[:HEAD_DIM] / o_aug[HEAD_DIM:HEAD_DIM + 1]
        if hh == HEADS_PER_STEP - 1:
            o_pair = [outs.pop((i, h2)) for h2 in range(HEADS_PER_STEP)]
            o_ref[0, rows(i), :] = jnp.concatenate(o_pair, axis=0).T.astype(o_ref.dtype)

    n_chain = nb * HEADS_PER_STEP
    for c in range(n_chain + ATTN_SLOTS - 1):
        if c < n_chain:
            pass_a(c // HEADS_PER_STEP, c % HEADS_PER_STEP)
        d = c - (ATTN_SLOTS - 1)
        if d >= 0:
            pass_b(d // HEADS_PER_STEP, d % HEADS_PER_STEP)


def _attention(tbl_t, qk, vt, bias_own, bias_prev):
    b, s, _ = qk.shape
    n_groups = ATT_W // LANES
    blk = (HEADS_PER_STEP, MOBA_BLOCK, MOBA_BLOCK)
    return pl.pallas_call(
        _attn_kernel,
        grid=(b, n_groups),
        in_specs=[pl.BlockSpec(memory_space=pltpu.SMEM),
                  pl.BlockSpec((1, s, LANES), lambda i, g: (i, 0, g)),
                  pl.BlockSpec((1, s, LANES), lambda i, g: (i, 0, n_groups + g)),
                  pl.BlockSpec((1, LANES, s), lambda i, g: (i, g, 0)),
                  pl.BlockSpec(blk, lambda i, g: (g, 0, 0)),
                  pl.BlockSpec(blk, lambda i, g: (g, 0, 0))],
        out_specs=pl.BlockSpec((1, s, LANES), lambda i, g: (i, 0, g)),
        out_shape=jax.ShapeDtypeStruct((b, s, ATT_W), BF16),
        scratch_shapes=[pltpu.VMEM((HEADS_PER_STEP, s, LANES), BF16),
                        pltpu.VMEM((HEADS_PER_STEP, HEAD_DIM + ONES_ROWS, s), BF16),
                        *[pltpu.VMEM((s, MOBA_BLOCK), F32)] * ATTN_SLOTS,
                        *[pltpu.VMEM((s, MOBA_BLOCK), BF16)] * ATTN_SLOTS],
        compiler_params=pltpu.CompilerParams(
            dimension_semantics=("arbitrary", "arbitrary"), vmem_limit_bytes=VMEM_LIMIT),
        name="moba_attn",
    )(tbl_t, qk, qk, vt, bias_own, bias_prev)


def _conv_kernel(rows_per_seq, cb_ref, a_ref, taps_ref, g_ref, beta_ref, o_ref, y_ref):
    g = pl.program_id(1)
    ct, rt, _ = a_ref.shape
    first_blk = lax.broadcasted_iota(jnp.int32, (rt, 1), 0) % rows_per_seq == 0
    for cl in range(ct):
        c = g * ct + cl
        taps = jnp.broadcast_to(taps_ref[cl:cl + 1, :], (CONV_BLK, 2 * CONV_BLK))
        w_c = pltpu.roll(taps, 0, 1, stride=1, stride_axis=0).astype(BF16)
        res = jnp.dot(a_ref[cl], w_c, preferred_element_type=F32)
        carry = pltpu.roll(res[:, CONV_BLK:], 1, 0)
        y = res[:, :CONV_BLK] + jnp.where(first_blk, 0.0, carry) + cb_ref[c]
        for r8 in range(0, rt, SUBLANES):
            y_ref[pl.ds(r8 * CONV_ROW_PITCH + c, SUBLANES, stride=CONV_ROW_PITCH), :] = y[r8:r8 + SUBLANES]

    @pl.when(g == pl.num_programs(1) - 1)
    def _():
        for r in range(rt):
            x = y_ref[r * CONV_ROW_PITCH:r * CONV_ROW_PITCH + CONV_CH, :]
            mu = jnp.mean(x, axis=0, keepdims=True)
            d = x - mu
            var = jnp.mean(d * d, axis=0, keepdims=True)
            z = (d * lax.rsqrt(var + LN_EPS)).T * g_ref[...] + beta_ref[...]
            o_ref[r * CONV_BLK:(r + 1) * CONV_BLK, :] = (z * jax.nn.sigmoid(z)).astype(o_ref.dtype)


def _conv_module(ut, conv_w, conv_b, ln_g, ln_b):
    b, c, s = ut.shape
    rows_per_seq = s // CONV_BLK
    n_rows = b * rows_per_seq
    row_tile = math.gcd(n_rows, CONV_ROW_TILE)
    assert row_tile % rows_per_seq == 0
    a = ut.reshape(b, c, rows_per_seq, CONV_BLK).transpose(1, 0, 2, 3).reshape(c, n_rows, CONV_BLK)
    taps = jnp.pad(conv_w[::-1].T.astype(F32), ((0, 0), (0, 2 * CONV_BLK - CONV_WIDTH)))
    return pl.pallas_call(
        functools.partial(_conv_kernel, rows_per_seq),
        grid=(n_rows // row_tile, c // CONV_CH_TILE),
        in_specs=[pl.BlockSpec(memory_space=pltpu.SMEM),
                  pl.BlockSpec((CONV_CH_TILE, row_tile, CONV_BLK), lambda i, g: (g, i, 0)),
                  pl.BlockSpec((CONV_CH_TILE, 2 * CONV_BLK), lambda i, g: (g, 0)),
                  _resident(ln_g.shape), _resident(ln_b.shape)],
        out_specs=pl.BlockSpec((row_tile * CONV_BLK, c), lambda i, g: (i, 0)),
        out_shape=jax.ShapeDtypeStruct((b * s, c), BF16),
        scratch_shapes=[pltpu.VMEM((row_tile * CONV_ROW_PITCH, CONV_BLK), F32)],
        compiler_params=pltpu.CompilerParams(
            dimension_semantics=("arbitrary", "arbitrary"), vmem_limit_bytes=VMEM_LIMIT),
        name="conv_module",
    )(conv_b.astype(F32), a, taps, ln_g, ln_b)


def _mix_ffn_kernel(alpha, x_ref, att_ref, cv_ref, gate_ref, p_ref,
                    wa_ref, wc_ref, wm_ref, g1_ref, b1_ref,
                    wg_ref, wu_ref, wd_ref, wpg_ref, bpg_ref, wp_ref, g2_ref, b2_ref, o_ref):
    d = x_ref.shape[1]
    y_att = jnp.dot(att_ref[...], wa_ref[...], preferred_element_type=F32)
    y_conv = jnp.dot(cv_ref[...], wc_ref[...], preferred_element_type=F32)
    merged = gate_ref[:, :d].astype(F32) * y_att + gate_ref[:, d:].astype(F32) * y_conv
    mixed = jnp.dot(merged.astype(BF16), wm_ref[...], preferred_element_type=F32)
    x1 = _layer_norm(alpha * x_ref[...] + mixed, g1_ref[...], b1_ref[...])

    xb = x1.astype(BF16)
    hg = jnp.dot(xb, wg_ref[...], preferred_element_type=F32)
    hu = jnp.dot(xb, wu_ref[...], preferred_element_type=F32)
    hid = (hg * jax.nn.sigmoid(hg) * hu).astype(BF16)
    ffn = jnp.dot(hid, wd_ref[...], preferred_element_type=F32)
    pg = jax.nn.sigmoid(jnp.dot(xb, wpg_ref[...], preferred_element_type=F32) + bpg_ref[...])
    pe = jnp.dot(p_ref[...].astype(BF16), wp_ref[...], preferred_element_type=F32)
    o_ref[...] = _layer_norm(alpha * x1 + ffn + pg * pe, g2_ref[...], b2_ref[...])


def _mix_ffn(alpha, x, att, cv, gates, p, weights, tm):
    t, d = x.shape
    row = lambda n: pl.BlockSpec((tm, n), lambda i: (i, 0))
    return pl.pallas_call(
        functools.partial(_mix_ffn_kernel, alpha),
        grid=(t // tm,),
        in_specs=[row(d), row(ATT_W), row(CONV_CH), row(2 * d), row(PLE_DIM)]
                 + [_resident(w.shape) for w in weights],
        out_specs=row(d),
        out_shape=jax.ShapeDtypeStruct((t, d), F32),
        compiler_params=pltpu.CompilerParams(
            dimension_semantics=("arbitrary",), vmem_limit_bytes=VMEM_LIMIT),
        name="mix_ffn",
    )(x, att, cv, gates, p, *weights)


def kernel(x, p, w_in, b_gate, bias_table, w_att_out, conv_w, conv_b, conv_ln_g, conv_ln_b,
           w_conv_out, w_mix_out, ln_mix_g, ln_mix_b, w_ffn_gate, w_ffn_up, w_ffn_down,
           w_ple, w_ple_gate, b_ple_gate, ln_ffn_g, ln_ffn_b):
    b, s, d = x.shape
    depth = w_in.shape[0]
    assert d == D_MODEL and s % MOBA_BLOCK == 0
    alpha = (2.0 * depth) ** 0.25
    tm = 512
    tm_ffn = 512
    row = lambda v: v.reshape(1, -1).astype(F32)
    k_end, v_end = 2 * ATT_W, 3 * ATT_W
    u_end = v_end + 2 * CONV_CH

    tbl_t = bias_table.T.astype(F32)
    bias_own, bias_prev = _bias_tables(tbl_t)
    for i in range(depth):
        w = w_in[i].astype(BF16)
        qk, vt, ut, gates = _proj(x, w[:, :k_end], w[:, k_end:v_end].T, w[:, v_end:u_end].T,
                                  w[:, u_end:], row(b_gate[i]), tm)
        att = _attention(tbl_t, qk, vt, bias_own, bias_prev)
        cv = _conv_module(ut, conv_w[i], conv_b[i], row(conv_ln_g[i]), row(conv_ln_b[i]))
        weights = (w_att_out[i].astype(BF16), w_conv_out[i].astype(BF16), w_mix_out[i].astype(BF16),
                   row(ln_mix_g[i]), row(ln_mix_b[i]),
                   w_ffn_gate[i].astype(BF16), w_ffn_up[i].astype(BF16), w_ffn_down[i].astype(BF16),
                   w_ple_gate[i].astype(BF16), row(b_ple_gate[i]), w_ple[i].astype(BF16),
                   row(ln_ffn_g[i]), row(ln_ffn_b[i]))
        x2 = _mix_ffn(alpha, x.reshape(b * s, d), att.reshape(b * s, ATT_W), cv,
                      gates.reshape(b * s, 2 * d), p[i].reshape(b * s, PLE_DIM), weights, tm_ffn)
        x = x2.reshape(b, s, d)
    return x
```

```python
import functools
import math

import numpy as np
import jax
import jax.numpy as jnp
from jax import lax
from jax.experimental import pallas as pl
from jax.experimental.pallas import tpu as pltpu

D_MODEL = 1024
PLE_DIM = 256
ATT_HEADS = 8
HEAD_DIM = 64
ATT_W = ATT_HEADS * HEAD_DIM
MOBA_BLOCK = 256
MOBA_TOPK = 3
CONV_CH = 512
CONV_WIDTH = 31
NUM_BUCKETS = 32
MAX_DISTANCE = 128
FFN_HIDDEN = 2816
LN_EPS = 1e-5
NEG_INF = -1e30

LANES = 128
SUBLANES = 8
HEADS_PER_STEP = LANES // HEAD_DIM
ONES_ROWS = 16
LOG2E = math.log2(math.e)
ATTN_SLOTS = 4
MIX_ROW_SPLIT = 2
CONV_BLK = LANES
CONV_ROW_TILE = 64
CONV_CH_TILE = 64
CONV_ROW_PITCH = CONV_CH + 8
VMEM_LIMIT = 56 * 1024 * 1024

BF16 = jnp.bfloat16
F32 = jnp.float32
NT_DIMS = (((1,), (1,)), ((), ()))


def _resident(shape):
    return pl.BlockSpec(shape, lambda *_: (0,) * len(shape), pipeline_mode=pl.Buffered(1))


def _layer_norm(h, g, b):
    mu = jnp.mean(h, axis=-1, keepdims=True)
    d = h - mu
    var = jnp.mean(d * d, axis=-1, keepdims=True)
    return d * lax.rsqrt(var + LN_EPS) * g + b


def _t5_bucket_np(rel):
    n = np.maximum(rel, 0)
    max_exact = NUM_BUCKETS // 2
    nf = np.maximum(n, 1).astype(np.float32)
    large = max_exact + (np.log(nf / np.float32(max_exact)) / np.float32(math.log(MAX_DISTANCE / max_exact))
                         * np.float32(NUM_BUCKETS - max_exact)).astype(np.int32)
    large = np.minimum(large, NUM_BUCKETS - 1)
    return np.where(n < max_exact, n, large).astype(np.int32)


def _bias_bucket_tables():
    kpos = np.arange(MOBA_BLOCK)[:, None]
    qpos = np.arange(MOBA_BLOCK)[None, :]
    rel_own = qpos - kpos
    own = np.where(rel_own >= 0, _t5_bucket_np(rel_own), NUM_BUCKETS)
    prev = _t5_bucket_np(rel_own + MOBA_BLOCK)
    return own.astype(np.int32), prev.astype(np.int32)


def _bias_kernel(tbl_ref, own_ix_ref, prev_ix_ref, own_ref, prev_ref):
    h = pl.program_id(0)
    own_ix = own_ix_ref[...]
    prev_ix = prev_ix_ref[...]
    own = jnp.full(own_ix.shape, NEG_INF, F32)
    prev = jnp.zeros(prev_ix.shape, F32)
    for b in range(NUM_BUCKETS):
        t = tbl_ref[h, b] * LOG2E
        own = jnp.where(own_ix == b, t, own)
        prev = jnp.where(prev_ix == b, t, prev)
    own_ref[0] = own
    prev_ref[0] = prev


def _bias_tables(tbl_t):
    own_ix, prev_ix = _bias_bucket_tables()
    blk = (MOBA_BLOCK, MOBA_BLOCK)
    return pl.pallas_call(
        _bias_kernel,
        grid=(ATT_HEADS,),
        in_specs=[pl.BlockSpec(memory_space=pltpu.SMEM),
                  pl.BlockSpec(blk, lambda h: (0, 0)),
                  pl.BlockSpec(blk, lambda h: (0, 0))],
        out_specs=[pl.BlockSpec((1,) + blk, lambda h: (h, 0, 0)),
                   pl.BlockSpec((1,) + blk, lambda h: (h, 0, 0))],
        out_shape=[jax.ShapeDtypeStruct((ATT_HEADS,) + blk, F32)] * 2,
        name="t5_bias",
    )(tbl_t, jnp.asarray(own_ix), jnp.asarray(prev_ix))


def _proj_kernel(x_ref, wqk_ref, wvt_ref, wagt_ref, wgate_ref, bgate_ref,
                 qk_ref, vt_ref, ut_ref, gate_ref):
    xb = x_ref[0].astype(BF16)
    qk = jnp.dot(xb, wqk_ref[...], preferred_element_type=F32)
    qk_ref[0, :, :ATT_W] = (qk[:, :ATT_W] * (LOG2E * HEAD_DIM ** -0.5)).astype(BF16)
    qk_ref[0, :, ATT_W:] = qk[:, ATT_W:].astype(BF16)
    vt = lax.dot_general(wvt_ref[...], xb, NT_DIMS, preferred_element_type=F32)
    vt_ref[0] = vt.astype(BF16)
    agt = lax.dot_general(wagt_ref[...], xb, NT_DIMS, preferred_element_type=F32)
    ut_ref[0] = (agt[:CONV_CH] * jax.nn.sigmoid(agt[CONV_CH:])).astype(BF16)
    gl = jnp.dot(xb, wgate_ref[...], preferred_element_type=F32) + bgate_ref[...]
    gate_ref[0] = jax.nn.sigmoid(gl).astype(BF16)


def _proj(x, wqk, wvt, wagt, wgate, bgate, tm):
    b, s, d = x.shape
    return pl.pallas_call(
        _proj_kernel,
        grid=(b, s // tm),
        in_specs=[pl.BlockSpec((1, tm, d), lambda i, j: (i, j, 0)),
                  _resident(wqk.shape), _resident(wvt.shape), _resident(wagt.shape),
                  _resident(wgate.shape), _resident(bgate.shape)],
        out_specs=[pl.BlockSpec((1, tm, 2 * ATT_W), lambda i, j: (i, j, 0)),
                   pl.BlockSpec((1, ATT_W, tm), lambda i, j: (i, 0, j)),
                   pl.BlockSpec((1, CONV_CH, tm), lambda i, j: (i, 0, j)),
                   pl.BlockSpec((1, tm, 2 * d), lambda i, j: (i, j, 0))],
        out_shape=[jax.ShapeDtypeStruct((b, s, 2 * ATT_W), BF16),
                   jax.ShapeDtypeStruct((b, ATT_W, s), BF16),
                   jax.ShapeDtypeStruct((b, CONV_CH, s), BF16),
                   jax.ShapeDtypeStruct((b, s, 2 * d), BF16)],
        compiler_params=pltpu.CompilerParams(
            dimension_semantics=("arbitrary", "arbitrary"), vmem_limit_bytes=VMEM_LIMIT),
        name="in_proj",
    )(x, wqk, wvt, wagt, wgate, bgate)


def _attn_kernel(tbl_ref, q_ref, k_ref, vt_ref, bown_ref, bprev_ref, o_ref,
                 qm_ref, vaug_ref, *sp_refs):
    s_refs, p_refs = sp_refs[:ATTN_SLOTS], sp_refs[ATTN_SLOTS:]
    hp = pl.program_id(1)
    s = q_ref.shape[1]
    nb = s // MOBA_BLOCK
    q = q_ref[0]
    k = k_ref[0]
    lane = lax.broadcasted_iota(jnp.int32, (1, LANES), 1)
    k_mean = (jnp.sum(k.astype(F32).reshape(nb, MOBA_BLOCK, LANES), axis=1)
              * (1.0 / MOBA_BLOCK)).astype(BF16)
    q_blk = lax.broadcasted_iota(jnp.int32, (1, s), 1) // MOBA_BLOCK
    blk = lax.broadcasted_iota(jnp.int32, (nb, 1), 0)
    past = blk < q_blk

    sel_neg, far_bias = [], []
    for hh in range(HEADS_PER_STEP):
        qm = jnp.where(lane // HEAD_DIM == hh, q, jnp.zeros_like(q))
        qm_ref[hh] = qm
        vaug_ref[hh, :HEAD_DIM, :] = vt_ref[0, hh * HEAD_DIM:(hh + 1) * HEAD_DIM, :]
        vaug_ref[hh, HEAD_DIM:, :] = jnp.ones((ONES_ROWS, s), BF16)
        gate = lax.dot_general(k_mean, qm, NT_DIMS, preferred_element_type=F32)
        rank = jnp.zeros((nb, s), jnp.int32)
        for m in range(nb):
            gm = gate[m:m + 1, :]
            beats = (gm > gate) | ((gm == gate) & (m < blk))
            rank = rank + jnp.where(beats & (m < q_blk), 1, 0)
        sel_neg.append(jnp.where(past & (rank < MOBA_TOPK), 0.0, NEG_INF).astype(F32))
        far_bias.append(tbl_ref[hp * HEADS_PER_STEP + hh, NUM_BUCKETS - 1] * LOG2E)

    rows = lambda j: slice(j * MOBA_BLOCK, (j + 1) * MOBA_BLOCK)
    slot_of = lambda i, hh: (i * HEADS_PER_STEP + hh) % ATTN_SLOTS
    shifts = {}
    outs = {}

    def pass_a(i, hh):
        hi = (i + 1) * MOBA_BLOCK
        slot = slot_of(i, hh)
        col_max, offs = None, []
        for j in range(i + 1):
            blk_t = lax.dot_general(k_ref[0, rows(j), :], qm_ref[hh, rows(i), :], NT_DIMS,
                                    preferred_element_type=F32)
            if j == i:
                blk_t = blk_t + bown_ref[hh]
                off = None
            elif j == i - 1:
                blk_t = blk_t + bprev_ref[hh]
                off = sel_neg[hh][j:j + 1, rows(i)]
            else:
                off = sel_neg[hh][j:j + 1, rows(i)] + far_bias[hh]
            s_refs[slot][rows(j), :] = blk_t
            cand = jnp.max(blk_t, axis=0, keepdims=True)
            cand = cand if off is None else cand + off
            col_max = cand if col_max is None else jnp.maximum(col_max, cand)
            offs.append(off)
        shifts[i, hh] = [col_max if off is None else col_max - off for off in offs]

    def pass_b(i, hh):
        hi = (i + 1) * MOBA_BLOCK
        slot = slot_of(i, hh)
        for j in range(i + 1):
            p_refs[slot][rows(j), :] = jnp.exp2(s_refs[slot][rows(j), :] - shifts[i, hh][j]).astype(BF16)
        o_aug = jnp.dot(vaug_ref[hh, :, :hi], p_refs[slot][:hi, :], preferred_element_type=F32)
        outs[i, hh] = o_aug[:HEAD_DIM] / o_aug[HEAD_DIM:HEAD_DIM + 1]
        if hh == HEADS_PER_STEP - 1:
            o_pair = [outs.pop((i, h2)) for h2 in range(HEADS_PER_STEP)]
            o_ref[0, rows(i), :] = jnp.concatenate(o_pair, axis=0).T.astype(o_ref.dtype)

    n_chain = nb * HEADS_PER_STEP
    for c in range(n_chain + ATTN_SLOTS - 1):
        if c < n_chain:
            pass_a(c // HEADS_PER_STEP, c % HEADS_PER_STEP)
        d = c - (ATTN_SLOTS - 1)
        if d >= 0:
            pass_b(d // HEADS_PER_STEP, d % HEADS_PER_STEP)


def _attention(tbl_t, qk, vt, bias_own, bias_prev):
    b, s, _ = qk.shape
    n_groups = ATT_W // LANES
    blk = (HEADS_PER_STEP, MOBA_BLOCK, MOBA_BLOCK)
    return pl.pallas_call(
        _attn_kernel,
        grid=(b, n_groups),
        in_specs=[pl.BlockSpec(memory_space=pltpu.SMEM),
                  pl.BlockSpec((1, s, LANES), lambda i, g: (i, 0, g)),
                  pl.BlockSpec((1, s, LANES), lambda i, g: (i, 0, n_groups + g)),
                  pl.BlockSpec((1, LANES, s), lambda i, g: (i, g, 0)),
                  pl.BlockSpec(blk, lambda i, g: (g, 0, 0)),
                  pl.BlockSpec(blk, lambda i, g: (g, 0, 0))],
        out_specs=pl.BlockSpec((1, s, LANES), lambda i, g: (i, 0, g)),
        out_shape=jax.ShapeDtypeStruct((b, s, ATT_W), BF16),
        scratch_shapes=[pltpu.VMEM((HEADS_PER_STEP, s, LANES), BF16),
                        pltpu.VMEM((HEADS_PER_STEP, HEAD_DIM + ONES_ROWS, s), BF16),
                        *[pltpu.VMEM((s, MOBA_BLOCK), F32)] * ATTN_SLOTS,
                        *[pltpu.VMEM((s, MOBA_BLOCK), BF16)] * ATTN_SLOTS],
        compiler_params=pltpu.CompilerParams(
            dimension_semantics=("arbitrary", "arbitrary"), vmem_limit_bytes=VMEM_LIMIT),
        name="moba_attn",
    )(tbl_t, qk, qk, vt, bias_own, bias_prev)


def _conv_kernel(rows_per_seq, cb_ref, a_ref, taps_ref, g_ref, beta_ref, o_ref, y_ref):
    g = pl.program_id(1)
    ct, rt, _ = a_ref.shape
    first_blk = lax.broadcasted_iota(jnp.int32, (rt, 1), 0) % rows_per_seq == 0
    for cl in range(ct):
        c = g * ct + cl
        taps = jnp.broadcast_to(taps_ref[cl:cl + 1, :], (CONV_BLK, 2 * CONV_BLK))
        w_c = pltpu.roll(taps, 0, 1, stride=1, stride_axis=0).astype(BF16)
        res = jnp.dot(a_ref[cl], w_c, preferred_element_type=F32)
        carry = pltpu.roll(res[:, CONV_BLK:], 1, 0)
        y = res[:, :CONV_BLK] + jnp.where(first_blk, 0.0, carry) + cb_ref[c]
        for r8 in range(0, rt, SUBLANES):
            y_ref[pl.ds(r8 * CONV_ROW_PITCH + c, SUBLANES, stride=CONV_ROW_PITCH), :] = y[r8:r8 + SUBLANES]

    @pl.when(g == pl.num_programs(1) - 1)
    def _():
        for r in range(rt):
            x = y_ref[r * CONV_ROW_PITCH:r * CONV_ROW_PITCH + CONV_CH, :]
            mu = jnp.mean(x, axis=0, keepdims=True)
            d = x - mu
            var = jnp.mean(d * d, axis=0, keepdims=True)
            z = (d * lax.rsqrt(var + LN_EPS)).T * g_ref[...] + beta_ref[...]
            o_ref[r * CONV_BLK:(r + 1) * CONV_BLK, :] = (z * jax.nn.sigmoid(z)).astype(o_ref.dtype)


def _conv_module(ut, conv_w, conv_b, ln_g, ln_b):
    b, c, s = ut.shape
    rows_per_seq = s // CONV_BLK
    n_rows = b * rows_per_seq
    row_tile = math.gcd(n_rows, CONV_ROW_TILE)
    assert row_tile % rows_per_seq == 0
    a = ut.reshape(b, c, rows_per_seq, CONV_BLK).transpose(1, 0, 2, 3).reshape(c, n_rows, CONV_BLK)
    taps = jnp.pad(conv_w[::-1].T.astype(F32), ((0, 0), (0, 2 * CONV_BLK - CONV_WIDTH)))
    return pl.pallas_call(
        functools.partial(_conv_kernel, rows_per_seq),
        grid=(n_rows // row_tile, c // CONV_CH_TILE),
        in_specs=[pl.BlockSpec(memory_space=pltpu.SMEM),
                  pl.BlockSpec((CONV_CH_TILE, row_tile, CONV_BLK), lambda i, g: (g, i, 0)),
                  pl.BlockSpec((CONV_CH_TILE, 2 * CONV_BLK), lambda i, g: (g, 0)),
                  _resident(ln_g.shape), _resident(ln_b.shape)],
        out_specs=pl.BlockSpec((row_tile * CONV_BLK, c), lambda i, g: (i, 0)),
        out_shape=jax.ShapeDtypeStruct((b * s, c), BF16),
        scratch_shapes=[pltpu.VMEM((row_tile * CONV_ROW_PITCH, CONV_BLK), F32)],
        compiler_params=pltpu.CompilerParams(
            dimension_semantics=("arbitrary", "arbitrary"), vmem_limit_bytes=VMEM_LIMIT),
        name="conv_module",
    )(conv_b.astype(F32), a, taps, ln_g, ln_b)


def _mix_ffn_kernel(alpha, x_ref, att_ref, cv_ref, gate_ref, p_ref,
                    wa_ref, wc_ref, wm_ref, g1_ref, b1_ref,
                    wg_ref, wu_ref, wd_ref, wpg_ref, bpg_ref, wp_ref, g2_ref, b2_ref, o_ref):
    tm, d = x_ref.shape
    for h in range(MIX_ROW_SPLIT):
        rs = slice(h * (tm // MIX_ROW_SPLIT), (h + 1) * (tm // MIX_ROW_SPLIT))
        y_att = jnp.dot(att_ref[rs, :], wa_ref[...], preferred_element_type=F32)
        y_conv = jnp.dot(cv_ref[rs, :], wc_ref[...], preferred_element_type=F32)
        merged = gate_ref[rs, :d].astype(F32) * y_att + gate_ref[rs, d:].astype(F32) * y_conv
        mixed = jnp.dot(merged.astype(BF16), wm_ref[...], preferred_element_type=F32)
        x1 = _layer_norm(alpha * x_ref[rs, :] + mixed, g1_ref[...], b1_ref[...])

        xb = x1.astype(BF16)
        hg = jnp.dot(xb, wg_ref[...], preferred_element_type=F32)
        hu = jnp.dot(xb, wu_ref[...], preferred_element_type=F32)
        hid = (hg * jax.nn.sigmoid(hg) * hu).astype(BF16)
        ffn = jnp.dot(hid, wd_ref[...], preferred_element_type=F32)
        pg = jax.nn.sigmoid(jnp.dot(xb, wpg_ref[...], preferred_element_type=F32) + bpg_ref[...])
        pe = jnp.dot(p_ref[rs, :].astype(BF16), wp_ref[...], preferred_element_type=F32)
        o_ref[rs, :] = _layer_norm(alpha * x1 + ffn + pg * pe, g2_ref[...], b2_ref[...])


def _mix_ffn(alpha, x, att, cv, gates, p, weights, tm):
    t, d = x.shape
    row = lambda n: pl.BlockSpec((tm, n), lambda i: (i, 0))
    return pl.pallas_call(
        functools.partial(_mix_ffn_kernel, alpha),
        grid=(t // tm,),
        in_specs=[row(d), row(ATT_W), row(CONV_CH), row(2 * d), row(PLE_DIM)]
                 + [_resident(w.shape) for w in weights],
        out_specs=row(d),
        out_shape=jax.ShapeDtypeStruct((t, d), F32),
        compiler_params=pltpu.CompilerParams(
            dimension_semantics=("arbitrary",), vmem_limit_bytes=VMEM_LIMIT),
        name="mix_ffn",
    )(x, att, cv, gates, p, *weights)


def kernel(x, p, w_in, b_gate, bias_table, w_att_out, conv_w, conv_b, conv_ln_g, conv_ln_b,
           w_conv_out, w_mix_out, ln_mix_g, ln_mix_b, w_ffn_gate, w_ffn_up, w_ffn_down,
           w_ple, w_ple_gate, b_ple_gate, ln_ffn_g, ln_ffn_b):
    b, s, d = x.shape
    depth = w_in.shape[0]
    assert d == D_MODEL and s % MOBA_BLOCK == 0
    alpha = (2.0 * depth) ** 0.25
    tm = 1024
    tm_ffn = 512
    row = lambda v: v.reshape(1, -1).astype(F32)
    k_end, v_end = 2 * ATT_W, 3 * ATT_W
    u_end = v_end + 2 * CONV_CH

    tbl_t = bias_table.T.astype(F32)
    bias_own, bias_prev = _bias_tables(tbl_t)
    for i in range(depth):
        w = w_in[i].astype(BF16)
        qk, vt, ut, gates = _proj(x, w[:, :k_end], w[:, k_end:v_end].T, w[:, v_end:u_end].T,
                                  w[:, u_end:], row(b_gate[i]), tm)
        att = _attention(tbl_t, qk, vt, bias_own, bias_prev)
        cv = _conv_module(ut, conv_w[i], conv_b[i], row(conv_ln_g[i]), row(conv_ln_b[i]))
        weights = (w_att_out[i].astype(BF16), w_conv_out[i].astype(BF16), w_mix_out[i].astype(BF16),
                   row(ln_mix_g[i]), row(ln_mix_b[i]),
                   w_ffn_gate[i].astype(BF16), w_ffn_up[i].astype(BF16), w_ffn_down[i].astype(BF16),
                   w_ple_gate[i].astype(BF16), row(b_ple_gate[i]), w_ple[i].astype(BF16),
                   row(ln_ffn_g[i]), row(ln_ffn_b[i]))
        x2 = _mix_ffn(alpha, x.reshape(b * s, d), att.reshape(b * s, ATT_W), cv,
                      gates.reshape(b * s, 2 * d), p[i].reshape(b * s, PLE_DIM), weights, tm_ffn)
        x = x2.reshape(b, s, d)
    return x
```

```python
import functools
import math

import numpy as np
import jax
import jax.numpy as jnp
from jax import lax
from jax.experimental import pallas as pl
from jax.experimental.pallas import tpu as pltpu

D_MODEL = 1024
PLE_DIM = 256
ATT_HEADS = 8
HEAD_DIM = 64
ATT_W = ATT_HEADS * HEAD_DIM
MOBA_BLOCK = 256
MOBA_TOPK = 3
CONV_CH = 512
CONV_WIDTH = 31
NUM_BUCKETS = 32
MAX_DISTANCE = 128
FFN_HIDDEN = 2816
LN_EPS = 1e-5
NEG_INF = -1e30

LANES = 128
SUBLANES = 8
HEADS_PER_STEP = LANES // HEAD_DIM
ONES_ROWS = 16
LOG2E = math.log2(math.e)
ATTN_SLOTS = 4
MIX_ROW_SPLIT = 2
CONV_BLK = LANES
CONV_ROW_TILE = 64
CONV_CH_TILE = 64
CONV_ROW_PITCH = CONV_CH + 8
VMEM_LIMIT = 56 * 1024 * 1024

BF16 = jnp.bfloat16
F32 = jnp.float32
NT_DIMS = (((1,), (1,)), ((), ()))


def _resident(shape):
    return pl.BlockSpec(shape, lambda *_: (0,) * len(shape), pipeline_mode=pl.Buffered(1))


def _layer_norm(h, g, b):
    mu = jnp.mean(h, axis=-1, keepdims=True)
    d = h - mu
    var = jnp.mean(d * d, axis=-1, keepdims=True)
    return d * lax.rsqrt(var + LN_EPS) * g + b


def _t5_bucket_np(rel):
    n = np.maximum(rel, 0)
    max_exact = NUM_BUCKETS // 2
    nf = np.maximum(n, 1).astype(np.float32)
    large = max_exact + (np.log(nf / np.float32(max_exact)) / np.float32(math.log(MAX_DISTANCE / max_exact))
                         * np.float32(NUM_BUCKETS - max_exact)).astype(np.int32)
    large = np.minimum(large, NUM_BUCKETS - 1)
    return np.where(n < max_exact, n, large).astype(np.int32)


def _bias_bucket_tables():
    kpos = np.arange(MOBA_BLOCK)[:, None]
    qpos = np.arange(MOBA_BLOCK)[None, :]
    rel_own = qpos - kpos
    own = np.where(rel_own >= 0, _t5_bucket_np(rel_own), NUM_BUCKETS)
    prev = _t5_bucket_np(rel_own + MOBA_BLOCK)
    return own.astype(np.int32), prev.astype(np.int32)


def _bias_kernel(tbl_ref, own_ix_ref, prev_ix_ref, own_ref, prev_ref):
    h = pl.program_id(0)
    own_ix = own_ix_ref[...]
    prev_ix = prev_ix_ref[...]
    own = jnp.full(own_ix.shape, NEG_INF, F32)
    prev = jnp.zeros(prev_ix.shape, F32)
    for b in range(NUM_BUCKETS):
        t = tbl_ref[h, b] * LOG2E
        own = jnp.where(own_ix == b, t, own)
        prev = jnp.where(prev_ix == b, t, prev)
    own_ref[0] = own
    prev_ref[0] = prev


def _bias_tables(tbl_t):
    own_ix, prev_ix = _bias_bucket_tables()
    blk = (MOBA_BLOCK, MOBA_BLOCK)
    return pl.pallas_call(
        _bias_kernel,
        grid=(ATT_HEADS,),
        in_specs=[pl.BlockSpec(memory_space=pltpu.SMEM),
                  pl.BlockSpec(blk, lambda h: (0, 0)),
                  pl.BlockSpec(blk, lambda h: (0, 0))],
        out_specs=[pl.BlockSpec((1,) + blk, lambda h: (h, 0, 0)),
                   pl.BlockSpec((1,) + blk, lambda h: (h, 0, 0))],
        out_shape=[jax.ShapeDtypeStruct((ATT_HEADS,) + blk, F32)] * 2,
        name="t5_bias",
    )(tbl_t, jnp.asarray(own_ix), jnp.asarray(prev_ix))


def _proj_kernel(x_ref, wqk_ref, wvt_ref, wagt_ref, wgate_ref, bgate_ref,
                 qk_ref, vt_ref, ut_ref, gate_ref):
    xb = x_ref[0].astype(BF16)
    qk = jnp.dot(xb, wqk_ref[...], preferred_element_type=F32)
    qk_ref[0, :, :ATT_W] = (qk[:, :ATT_W] * (LOG2E * HEAD_DIM ** -0.5)).astype(BF16)
    qk_ref[0, :, ATT_W:] = qk[:, ATT_W:].astype(BF16)
    vt = lax.dot_general(wvt_ref[...], xb, NT_DIMS, preferred_element_type=F32)
    vt_ref[0] = vt.astype(BF16)
    agt = lax.dot_general(wagt_ref[...], xb, NT_DIMS, preferred_element_type=F32)
    ut_ref[0] = (agt[:CONV_CH] * jax.nn.sigmoid(agt[CONV_CH:])).astype(BF16)
    gl = jnp.dot(xb, wgate_ref[...], preferred_element_type=F32) + bgate_ref[...]
    gate_ref[0] = jax.nn.sigmoid(gl).astype(BF16)


def _proj(x, wqk, wvt, wagt, wgate, bgate, tm):
    b, s, d = x.shape
    return pl.pallas_call(
        _proj_kernel,
        grid=(b, s // tm),
        in_specs=[pl.BlockSpec((1, tm, d), lambda i, j: (i, j, 0)),
                  _resident(wqk.shape), _resident(wvt.shape), _resident(wagt.shape),
                  _resident(wgate.shape), _resident(bgate.shape)],
        out_specs=[pl.BlockSpec((1, tm, 2 * ATT_W), lambda i, j: (i, j, 0)),
                   pl.BlockSpec((1, ATT_W, tm), lambda i, j: (i, 0, j)),
                   pl.BlockSpec((1, CONV_CH, tm), lambda i, j: (i, 0, j)),
                   pl.BlockSpec((1, tm, 2 * d), lambda i, j: (i, j, 0))],
        out_shape=[jax.ShapeDtypeStruct((b, s, 2 * ATT_W), BF16),
                   jax.ShapeDtypeStruct((b, ATT_W, s), BF16),
                   jax.ShapeDtypeStruct((b, CONV_CH, s), BF16),
                   jax.ShapeDtypeStruct((b, s, 2 * d), BF16)],
        compiler_params=pltpu.CompilerParams(
            dimension_semantics=("arbitrary", "arbitrary"), vmem_limit_bytes=VMEM_LIMIT),
        name="in_proj",
    )(x, wqk, wvt, wagt, wgate, bgate)


def _attn_kernel(tbl_ref, q_ref, k_ref, vt_ref, bown_ref, bprev_ref, o_ref,
                 qm_ref, vaug_ref, *sp_refs):
    s_refs, p_refs = sp_refs[:ATTN_SLOTS], sp_refs[ATTN_SLOTS:]
    hp = pl.program_id(0)
    s = q_ref.shape[1]
    nb = s // MOBA_BLOCK
    q = q_ref[0]
    k = k_ref[0]
    lane = lax.broadcasted_iota(jnp.int32, (1, LANES), 1)
    k_mean = (jnp.sum(k.astype(F32).reshape(nb, MOBA_BLOCK, LANES), axis=1)
              * (1.0 / MOBA_BLOCK)).astype(BF16)
    q_blk = lax.broadcasted_iota(jnp.int32, (1, s), 1) // MOBA_BLOCK
    blk = lax.broadcasted_iota(jnp.int32, (nb, 1), 0)
    past = blk < q_blk

    sel_neg, far_bias = [], []
    for hh in range(HEADS_PER_STEP):
        qm = jnp.where(lane // HEAD_DIM == hh, q, jnp.zeros_like(q))
        qm_ref[hh] = qm
        vaug_ref[hh, :HEAD_DIM, :] = vt_ref[0, hh * HEAD_DIM:(hh + 1) * HEAD_DIM, :]
        vaug_ref[hh, HEAD_DIM:, :] = jnp.ones((ONES_ROWS, s), BF16)
        gate = lax.dot_general(k_mean, qm, NT_DIMS, preferred_element_type=F32)
        rank = jnp.zeros((nb, s), jnp.int32)
        for m in range(nb):
            gm = gate[m:m + 1, :]
            beats = (gm > gate) | ((gm == gate) & (m < blk))
            rank = rank + jnp.where(beats & (m < q_blk), 1, 0)
        sel_neg.append(jnp.where(past & (rank < MOBA_TOPK), 0.0, NEG_INF).astype(F32))
        far_bias.append(tbl_ref[hp * HEADS_PER_STEP + hh, NUM_BUCKETS - 1] * LOG2E)

    rows = lambda j: slice(j * MOBA_BLOCK, (j + 1) * MOBA_BLOCK)
    slot_of = lambda i, hh: (i * HEADS_PER_STEP + hh) % ATTN_SLOTS
    shifts = {}
    outs = {}

    def pass_a(i, hh):
        hi = (i + 1) * MOBA_BLOCK
        slot = slot_of(i, hh)
        st = lax.dot_general(k_ref[0, :hi, :], qm_ref[hh, rows(i), :], NT_DIMS,
                             preferred_element_type=F32)
        col_max, offs = None, []
        for j in range(i + 1):
            blk_t = st[rows(j)]
            if j == i:
                blk_t = blk_t + bown_ref[hh]
                off = None
            elif j == i - 1:
                blk_t = blk_t + bprev_ref[hh]
                off = sel_neg[hh][j:j + 1, rows(i)]
            else:
                off = sel_neg[hh][j:j + 1, rows(i)] + far_bias[hh]
            s_refs[slot][rows(j), :] = blk_t
            cand = jnp.max(blk_t, axis=0, keepdims=True)
            cand = cand if off is None else cand + off
            col_max = cand if col_max is None else jnp.maximum(col_max, cand)
            offs.append(off)
        shifts[i, hh] = [col_max if off is None else col_max - off for off in offs]

    def pass_b(i, hh):
        hi = (i + 1) * MOBA_BLOCK
        slot = slot_of(i, hh)
        for j in range(i + 1):
            p_refs[slot][rows(j), :] = jnp.exp2(s_refs[slot][rows(j), :] - shifts[i, hh][j]).astype(BF16)
        o_aug = jnp.dot(vaug_ref[hh, :, :hi], p_refs[slot][:hi, :], preferred_element_type=F32)
        outs[i, hh] = o_aug[:HEAD_DIM] / o_aug[HEAD_DIM:HEAD_DIM + 1]
        if hh == HEADS_PER_STEP - 1:
            o_pair = [outs.pop((i, h2)) for h2 in range(HEADS_PER_STEP)]
            o_ref[0, rows(i), :] = jnp.concatenate(o_pair, axis=0).T.astype(o_ref.dtype)

    n_chain = nb * HEADS_PER_STEP
    for c in range(n_chain + ATTN_SLOTS - 1):
        if c < n_chain:
            pass_a(c // HEADS_PER_STEP, c % HEADS_PER_STEP)
        d = c - (ATTN_SLOTS - 1)
        if d >= 0:
            pass_b(d // HEADS_PER_STEP, d % HEADS_PER_STEP)


def _attention(tbl_t, qk, vt, bias_own, bias_prev):
    b, s, _ = qk.shape
    n_groups = ATT_W // LANES
    blk = (HEADS_PER_STEP, MOBA_BLOCK, MOBA_BLOCK)
    return pl.pallas_call(
        _attn_kernel,
        grid=(n_groups, b),
        in_specs=[pl.BlockSpec(memory_space=pltpu.SMEM),
                  pl.BlockSpec((1, s, LANES), lambda g, i: (i, 0, g)),
                  pl.BlockSpec((1, s, LANES), lambda g, i: (i, 0, n_groups + g)),
                  pl.BlockSpec((1, LANES, s), lambda g, i: (i, g, 0)),
                  pl.BlockSpec(blk, lambda g, i: (g, 0, 0)),
                  pl.BlockSpec(blk, lambda g, i: (g, 0, 0))],
        out_specs=pl.BlockSpec((1, s, LANES), lambda g, i: (i, 0, g)),
        out_shape=jax.ShapeDtypeStruct((b, s, ATT_W), BF16),
        scratch_shapes=[pltpu.VMEM((HEADS_PER_STEP, s, LANES), BF16),
                        pltpu.VMEM((HEADS_PER_STEP, HEAD_DIM + ONES_ROWS, s), BF16),
                        *[pltpu.VMEM((s, MOBA_BLOCK), F32)] * ATTN_SLOTS,
                        *[pltpu.VMEM((s, MOBA_BLOCK), BF16)] * ATTN_SLOTS],
        compiler_params=pltpu.CompilerParams(
            dimension_semantics=("arbitrary", "arbitrary"), vmem_limit_bytes=VMEM_LIMIT),
        name="moba_attn",
    )(tbl_t, qk, qk, vt, bias_own, bias_prev)


def _conv_kernel(rows_per_seq, cb_ref, a_ref, taps_ref, g_ref, beta_ref, o_ref, y_ref):
    g = pl.program_id(1)
    ct, rt, _ = a_ref.shape
    first_blk = lax.broadcasted_iota(jnp.int32, (rt, 1), 0) % rows_per_seq == 0
    for cl in range(ct):
        c = g * ct + cl
        taps = jnp.broadcast_to(taps_ref[cl:cl + 1, :], (CONV_BLK, 2 * CONV_BLK))
        w_c = pltpu.roll(taps, 0, 1, stride=1, stride_axis=0).astype(BF16)
        res = jnp.dot(a_ref[cl], w_c, preferred_element_type=F32)
        carry = pltpu.roll(res[:, CONV_BLK:], 1, 0)
        y = res[:, :CONV_BLK] + jnp.where(first_blk, 0.0, carry) + cb_ref[c]
        for r8 in range(0, rt, SUBLANES):
            y_ref[pl.ds(r8 * CONV_ROW_PITCH + c, SUBLANES, stride=CONV_ROW_PITCH), :] = y[r8:r8 + SUBLANES]

    @pl.when(g == pl.num_programs(1) - 1)
    def _():
        for r in range(rt):
            x = y_ref[r * CONV_ROW_PITCH:r * CONV_ROW_PITCH + CONV_CH, :]
            mu = jnp.mean(x, axis=0, keepdims=True)
            d = x - mu
            var = jnp.mean(d * d, axis=0, keepdims=True)
            z = (d * lax.rsqrt(var + LN_EPS)).T * g_ref[...] + beta_ref[...]
            o_ref[r * CONV_BLK:(r + 1) * CONV_BLK, :] = (z * jax.nn.sigmoid(z)).astype(o_ref.dtype)


def _conv_module(ut, conv_w, conv_b, ln_g, ln_b):
    b, c, s = ut.shape
    rows_per_seq = s // CONV_BLK
    n_rows = b * rows_per_seq
    row_tile = math.gcd(n_rows, CONV_ROW_TILE)
    assert row_tile % rows_per_seq == 0
    a = ut.reshape(b, c, rows_per_seq, CONV_BLK).transpose(1, 0, 2, 3).reshape(c, n_rows, CONV_BLK)
    taps = jnp.pad(conv_w[::-1].T.astype(F32), ((0, 0), (0, 2 * CONV_BLK - CONV_WIDTH)))
    return pl.pallas_call(
        functools.partial(_conv_kernel, rows_per_seq),
        grid=(n_rows // row_tile, c // CONV_CH_TILE),
        in_specs=[pl.BlockSpec(memory_space=pltpu.SMEM),
                  pl.BlockSpec((CONV_CH_TILE, row_tile, CONV_BLK), lambda i, g: (g, i, 0)),
                  pl.BlockSpec((CONV_CH_TILE, 2 * CONV_BLK), lambda i, g: (g, 0)),
                  _resident(ln_g.shape), _resident(ln_b.shape)],
        out_specs=pl.BlockSpec((row_tile * CONV_BLK, c), lambda i, g: (i, 0)),
        out_shape=jax.ShapeDtypeStruct((b * s, c), BF16),
        scratch_shapes=[pltpu.VMEM((row_tile * CONV_ROW_PITCH, CONV_BLK), F32)],
        compiler_params=pltpu.CompilerParams(
            dimension_semantics=("arbitrary", "arbitrary"), vmem_limit_bytes=VMEM_LIMIT),
        name="conv_module",
    )(conv_b.astype(F32), a, taps, ln_g, ln_b)


def _mix_ffn_kernel(alpha, x_ref, att_ref, cv_ref, gate_ref, p_ref,
                    wa_ref, wc_ref, wm_ref, g1_ref, b1_ref,
                    wg_ref, wu_ref, wd_ref, wpg_ref, bpg_ref, wp_ref, g2_ref, b2_ref, o_ref):
    tm, d = x_ref.shape
    for h in range(MIX_ROW_SPLIT):
        rs = slice(h * (tm // MIX_ROW_SPLIT), (h + 1) * (tm // MIX_ROW_SPLIT))
        y_att = jnp.dot(att_ref[rs, :], wa_ref[...], preferred_element_type=F32)
        y_conv = jnp.dot(cv_ref[rs, :], wc_ref[...], preferred_element_type=F32)
        merged = gate_ref[rs, :d].astype(F32) * y_att + gate_ref[rs, d:].astype(F32) * y_conv
        mixed = jnp.dot(merged.astype(BF16), wm_ref[...], preferred_element_type=F32)
        x1 = _layer_norm(alpha * x_ref[rs, :] + mixed, g1_ref[...], b1_ref[...])

        xb = x1.astype(BF16)
        hg = jnp.dot(xb, wg_ref[...], preferred_element_type=F32)
        hu = jnp.dot(xb, wu_ref[...], preferred_element_type=F32)
        hid = (hg * jax.nn.sigmoid(hg) * hu).astype(BF16)
        ffn = jnp.dot(hid, wd_ref[...], preferred_element_type=F32)
        pg = jax.nn.sigmoid(jnp.dot(xb, wpg_ref[...], preferred_element_type=F32) + bpg_ref[...])
        pe = jnp.dot(p_ref[rs, :].astype(BF16), wp_ref[...], preferred_element_type=F32)
        o_ref[rs, :] = _layer_norm(alpha * x1 + ffn + pg * pe, g2_ref[...], b2_ref[...])


def _mix_ffn(alpha, x, att, cv, gates, p, weights, tm):
    t, d = x.shape
    row = lambda n: pl.BlockSpec((tm, n), lambda i: (i, 0))
    return pl.pallas_call(
        functools.partial(_mix_ffn_kernel, alpha),
        grid=(t // tm,),
        in_specs=[row(d), row(ATT_W), row(CONV_CH), row(2 * d), row(PLE_DIM)]
                 + [_resident(w.shape) for w in weights],
        out_specs=row(d),
        out_shape=jax.ShapeDtypeStruct((t, d), F32),
        compiler_params=pltpu.CompilerParams(
            dimension_semantics=("arbitrary",), vmem_limit_bytes=VMEM_LIMIT),
        name="mix_ffn",
    )(x, att, cv, gates, p, *weights)


def kernel(x, p, w_in, b_gate, bias_table, w_att_out, conv_w, conv_b, conv_ln_g, conv_ln_b,
           w_conv_out, w_mix_out, ln_mix_g, ln_mix_b, w_ffn_gate, w_ffn_up, w_ffn_down,
           w_ple, w_ple_gate, b_ple_gate, ln_ffn_g, ln_ffn_b):
    b, s, d = x.shape
    depth = w_in.shape[0]
    assert d == D_MODEL and s % MOBA_BLOCK == 0
    alpha = (2.0 * depth) ** 0.25
    tm = 1024
    tm_ffn = 512
    row = lambda v: v.reshape(1, -1).astype(F32)
    k_end, v_end = 2 * ATT_W, 3 * ATT_W
    u_end = v_end + 2 * CONV_CH

    tbl_t = bias_table.T.astype(F32)
    bias_own, bias_prev = _bias_tables(tbl_t)
    for i in range(depth):
        w = w_in[i].astype(BF16)
        qk, vt, ut, gates = _proj(x, w[:, :k_end], w[:, k_end:v_end].T, w[:, v_end:u_end].T,
                                  w[:, u_end:], row(b_gate[i]), tm)
        att = _attention(tbl_t, qk, vt, bias_own, bias_prev)
        cv = _conv_module(ut, conv_w[i], conv_b[i], row(conv_ln_g[i]), row(conv_ln_b[i]))
        weights = (w_att_out[i].astype(BF16), w_conv_out[i].astype(BF16), w_mix_out[i].astype(BF16),
                   row(ln_mix_g[i]), row(ln_mix_b[i]),
                   w_ffn_gate[i].astype(BF16), w_ffn_up[i].astype(BF16), w_ffn_down[i].astype(BF16),
                   w_ple_gate[i].astype(BF16), row(b_ple_gate[i]), w_ple[i].astype(BF16),
                   row(ln_ffn_g[i]), row(ln_ffn_b[i]))
        x2 = _mix_ffn(alpha, x.reshape(b * s, d), att.reshape(b * s, ATT_W), cv,
                      gates.reshape(b * s, 2 * d), p[i].reshape(b * s, PLE_DIM), weights, tm_ffn)
        x = x2.reshape(b, s, d)
    return x
```

```python
import functools
import math

import numpy as np
import jax
import jax.numpy as jnp
from jax import lax
from jax.experimental import pallas as pl
from jax.experimental.pallas import tpu as pltpu

D_MODEL = 1024
PLE_DIM = 256
ATT_HEADS = 8
HEAD_DIM = 64
ATT_W = ATT_HEADS * HEAD_DIM
MOBA_BLOCK = 256
MOBA_TOPK = 3
CONV_CH = 512
CONV_WIDTH = 31
NUM_BUCKETS = 32
MAX_DISTANCE = 128
FFN_HIDDEN = 2816
LN_EPS = 1e-5
NEG_INF = -1e30

LANES = 128
SUBLANES = 8
HEADS_PER_STEP = LANES // HEAD_DIM
ONES_ROWS = 16
LOG2E = math.log2(math.e)
ATTN_SLOTS = 4
ATTN_ROWS = 2
MIX_ROW_SPLIT = 2
CONV_BLK = LANES
CONV_ROW_TILE = 64
CONV_CH_TILE = 64
CONV_ROW_PITCH = CONV_CH + 8
VMEM_LIMIT = 56 * 1024 * 1024

BF16 = jnp.bfloat16
F32 = jnp.float32
NT_DIMS = (((1,), (1,)), ((), ()))


def _resident(shape):
    return pl.BlockSpec(shape, lambda *_: (0,) * len(shape), pipeline_mode=pl.Buffered(1))


def _layer_norm(h, g, b):
    mu = jnp.mean(h, axis=-1, keepdims=True)
    d = h - mu
    var = jnp.mean(d * d, axis=-1, keepdims=True)
    return d * lax.rsqrt(var + LN_EPS) * g + b


def _t5_bucket_np(rel):
    n = np.maximum(rel, 0)
    max_exact = NUM_BUCKETS // 2
    nf = np.maximum(n, 1).astype(np.float32)
    large = max_exact + (np.log(nf / np.float32(max_exact)) / np.float32(math.log(MAX_DISTANCE / max_exact))
                         * np.float32(NUM_BUCKETS - max_exact)).astype(np.int32)
    large = np.minimum(large, NUM_BUCKETS - 1)
    return np.where(n < max_exact, n, large).astype(np.int32)


def _bias_bucket_tables():
    kpos = np.arange(MOBA_BLOCK)[:, None]
    qpos = np.arange(MOBA_BLOCK)[None, :]
    rel_own = qpos - kpos
    own = np.where(rel_own >= 0, _t5_bucket_np(rel_own), NUM_BUCKETS)
    prev = _t5_bucket_np(rel_own + MOBA_BLOCK)
    return own.astype(np.int32), prev.astype(np.int32)


def _bias_kernel(tbl_ref, own_ix_ref, prev_ix_ref, own_ref, prev_ref):
    h = pl.program_id(0)
    own_ix = own_ix_ref[...]
    prev_ix = prev_ix_ref[...]
    own = jnp.full(own_ix.shape, NEG_INF, F32)
    prev = jnp.zeros(prev_ix.shape, F32)
    for b in range(NUM_BUCKETS):
        t = tbl_ref[h, b] * LOG2E
        own = jnp.where(own_ix == b, t, own)
        prev = jnp.where(prev_ix == b, t, prev)
    own_ref[0] = own
    prev_ref[0] = prev


def _bias_tables(tbl_t):
    own_ix, prev_ix = _bias_bucket_tables()
    blk = (MOBA_BLOCK, MOBA_BLOCK)
    return pl.pallas_call(
        _bias_kernel,
        grid=(ATT_HEADS,),
        in_specs=[pl.BlockSpec(memory_space=pltpu.SMEM),
                  pl.BlockSpec(blk, lambda h: (0, 0)),
                  pl.BlockSpec(blk, lambda h: (0, 0))],
        out_specs=[pl.BlockSpec((1,) + blk, lambda h: (h, 0, 0)),
                   pl.BlockSpec((1,) + blk, lambda h: (h, 0, 0))],
        out_shape=[jax.ShapeDtypeStruct((ATT_HEADS,) + blk, F32)] * 2,
        name="t5_bias",
    )(tbl_t, jnp.asarray(own_ix), jnp.asarray(prev_ix))


def _proj_kernel(x_ref, wqk_ref, wvt_ref, wagt_ref, wgate_ref, bgate_ref,
                 qk_ref, vt_ref, ut_ref, gate_ref):
    xb = x_ref[0].astype(BF16)
    qk = jnp.dot(xb, wqk_ref[...], preferred_element_type=F32)
    qk_ref[0, :, :ATT_W] = (qk[:, :ATT_W] * (LOG2E * HEAD_DIM ** -0.5)).astype(BF16)
    qk_ref[0, :, ATT_W:] = qk[:, ATT_W:].astype(BF16)
    vt = lax.dot_general(wvt_ref[...], xb, NT_DIMS, preferred_element_type=F32)
    vt_ref[0] = vt.astype(BF16)
    agt = lax.dot_general(wagt_ref[...], xb, NT_DIMS, preferred_element_type=F32)
    ut_ref[0] = (agt[:CONV_CH] * jax.nn.sigmoid(agt[CONV_CH:])).astype(BF16)
    gl = jnp.dot(xb, wgate_ref[...], preferred_element_type=F32) + bgate_ref[...]
    gate_ref[0] = jax.nn.sigmoid(gl).astype(BF16)


def _proj(x, wqk, wvt, wagt, wgate, bgate, tm):
    b, s, d = x.shape
    return pl.pallas_call(
        _proj_kernel,
        grid=(b, s // tm),
        in_specs=[pl.BlockSpec((1, tm, d), lambda i, j: (i, j, 0)),
                  _resident(wqk.shape), _resident(wvt.shape), _resident(wagt.shape),
                  _resident(wgate.shape), _resident(bgate.shape)],
        out_specs=[pl.BlockSpec((1, tm, 2 * ATT_W), lambda i, j: (i, j, 0)),
                   pl.BlockSpec((1, ATT_W, tm), lambda i, j: (i, 0, j)),
                   pl.BlockSpec((1, CONV_CH, tm), lambda i, j: (i, 0, j)),
                   pl.BlockSpec((1, tm, 2 * d), lambda i, j: (i, j, 0))],
        out_shape=[jax.ShapeDtypeStruct((b, s, 2 * ATT_W), BF16),
                   jax.ShapeDtypeStruct((b, ATT_W, s), BF16),
                   jax.ShapeDtypeStruct((b, CONV_CH, s), BF16),
                   jax.ShapeDtypeStruct((b, s, 2 * d), BF16)],
        compiler_params=pltpu.CompilerParams(
            dimension_semantics=("arbitrary", "arbitrary"), vmem_limit_bytes=VMEM_LIMIT),
        name="in_proj",
    )(x, wqk, wvt, wagt, wgate, bgate)


def _attn_kernel(tbl_ref, q_ref, k_ref, vt_ref, bown_ref, bprev_ref, o_ref,
                 qm_ref, vaug_ref, *sp_refs):
    s_refs, p_refs = sp_refs[:ATTN_SLOTS], sp_refs[ATTN_SLOTS:]
    hp = pl.program_id(0)
    n_rows, s, _ = q_ref.shape
    nb = s // MOBA_BLOCK
    lane = lax.broadcasted_iota(jnp.int32, (1, LANES), 1)
    q_blk = lax.broadcasted_iota(jnp.int32, (1, s), 1) // MOBA_BLOCK
    blk = lax.broadcasted_iota(jnp.int32, (nb, 1), 0)
    past = blk < q_blk
    far_bias = [tbl_ref[hp * HEADS_PER_STEP + hh, NUM_BUCKETS - 1] * LOG2E for hh in range(HEADS_PER_STEP)]

    sel_neg = {}
    for r in range(n_rows):
        q = q_ref[r]
        k = k_ref[r]
        k_mean = (jnp.sum(k.astype(F32).reshape(nb, MOBA_BLOCK, LANES), axis=1)
                  * (1.0 / MOBA_BLOCK)).astype(BF16)
        for hh in range(HEADS_PER_STEP):
            rh = r * HEADS_PER_STEP + hh
            qm = jnp.where(lane // HEAD_DIM == hh, q, jnp.zeros_like(q))
            qm_ref[rh] = qm
            vaug_ref[rh, :HEAD_DIM, :] = vt_ref[r, hh * HEAD_DIM:(hh + 1) * HEAD_DIM, :]
            vaug_ref[rh, HEAD_DIM:, :] = jnp.ones((ONES_ROWS, s), BF16)
            gate = lax.dot_general(k_mean, qm, NT_DIMS, preferred_element_type=F32)
            rank = jnp.zeros((nb, s), jnp.int32)
            for m in range(nb):
                gm = gate[m:m + 1, :]
                beats = (gm > gate) | ((gm == gate) & (m < blk))
                rank = rank + jnp.where(beats & (m < q_blk), 1, 0)
            sel_neg[r, hh] = jnp.where(past & (rank < MOBA_TOPK), 0.0, NEG_INF).astype(F32)

    rows = lambda j: slice(j * MOBA_BLOCK, (j + 1) * MOBA_BLOCK)
    chains = [(i, r, hh) for i in range(nb) for r in range(n_rows) for hh in range(HEADS_PER_STEP)]
    slot_of = {chain: c % ATTN_SLOTS for c, chain in enumerate(chains)}
    shifts = {}
    outs = {}

    def pass_a(i, r, hh):
        hi = (i + 1) * MOBA_BLOCK
        slot = slot_of[i, r, hh]
        st = lax.dot_general(k_ref[r, :hi, :], qm_ref[r * HEADS_PER_STEP + hh, rows(i), :], NT_DIMS,
                             preferred_element_type=F32)
        col_max, offs = None, []
        for j in range(i + 1):
            blk_t = st[rows(j)]
            if j == i:
                blk_t = blk_t + bown_ref[hh]
                off = None
            elif j == i - 1:
                blk_t = blk_t + bprev_ref[hh]
                off = sel_neg[r, hh][j:j + 1, rows(i)]
            else:
                off = sel_neg[r, hh][j:j + 1, rows(i)] + far_bias[hh]
            s_refs[slot][rows(j), :] = blk_t
            cand = jnp.max(blk_t, axis=0, keepdims=True)
            cand = cand if off is None else cand + off
            col_max = cand if col_max is None else jnp.maximum(col_max, cand)
            offs.append(off)
        shifts[i, r, hh] = [col_max if off is None else col_max - off for off in offs]

    def pass_b(i, r, hh):
        hi = (i + 1) * MOBA_BLOCK
        slot = slot_of[i, r, hh]
        shift = shifts.pop((i, r, hh))
        for j in range(i + 1):
            p_refs[slot][rows(j), :] = jnp.exp2(s_refs[slot][rows(j), :] - shift[j]).astype(BF16)
        o_aug = jnp.dot(vaug_ref[r * HEADS_PER_STEP + hh, :, :hi], p_refs[slot][:hi, :],
                        preferred_element_type=F32)
        outs[hh] = o_aug[:HEAD_DIM] / o_aug[HEAD_DIM:HEAD_DIM + 1]
        if hh == HEADS_PER_STEP - 1:
            o_pair = [outs.pop(h2) for h2 in range(HEADS_PER_STEP)]
            o_ref[r, rows(i), :] = jnp.concatenate(o_pair, axis=0).T.astype(o_ref.dtype)

    for c in range(len(chains) + ATTN_SLOTS - 1):
        if c < len(chains):
            pass_a(*chains[c])
        if c >= ATTN_SLOTS - 1:
            pass_b(*chains[c - (ATTN_SLOTS - 1)])


def _attention(tbl_t, qk, vt, bias_own, bias_prev):
    b, s, _ = qk.shape
    n_groups = ATT_W // LANES
    n_rows = math.gcd(b, ATTN_ROWS)
    blk = (HEADS_PER_STEP, MOBA_BLOCK, MOBA_BLOCK)
    return pl.pallas_call(
        _attn_kernel,
        grid=(n_groups, b // n_rows),
        in_specs=[pl.BlockSpec(memory_space=pltpu.SMEM),
                  pl.BlockSpec((n_rows, s, LANES), lambda g, i: (i, 0, g)),
                  pl.BlockSpec((n_rows, s, LANES), lambda g, i: (i, 0, n_groups + g)),
                  pl.BlockSpec((n_rows, LANES, s), lambda g, i: (i, g, 0)),
                  pl.BlockSpec(blk, lambda g, i: (g, 0, 0)),
                  pl.BlockSpec(blk, lambda g, i: (g, 0, 0))],
        out_specs=pl.BlockSpec((n_rows, s, LANES), lambda g, i: (i, 0, g)),
        out_shape=jax.ShapeDtypeStruct((b, s, ATT_W), BF16),
        scratch_shapes=[pltpu.VMEM((n_rows * HEADS_PER_STEP, s, LANES), BF16),
                        pltpu.VMEM((n_rows * HEADS_PER_STEP, HEAD_DIM + ONES_ROWS, s), BF16),
                        *[pltpu.VMEM((s, MOBA_BLOCK), F32)] * ATTN_SLOTS,
                        *[pltpu.VMEM((s, MOBA_BLOCK), BF16)] * ATTN_SLOTS],
        compiler_params=pltpu.CompilerParams(
            dimension_semantics=("arbitrary", "arbitrary"), vmem_limit_bytes=VMEM_LIMIT),
        name="moba_attn",
    )(tbl_t, qk, qk, vt, bias_own, bias_prev)


def _conv_kernel(rows_per_seq, cb_ref, a_ref, taps_ref, g_ref, beta_ref, o_ref, y_ref):
    g = pl.program_id(1)
    ct, rt, _ = a_ref.shape
    first_blk = lax.broadcasted_iota(jnp.int32, (rt, 1), 0) % rows_per_seq == 0
    for cl in range(ct):
        c = g * ct + cl
        taps = jnp.broadcast_to(taps_ref[cl:cl + 1, :], (CONV_BLK, 2 * CONV_BLK))
        w_c = pltpu.roll(taps, 0, 1, stride=1, stride_axis=0).astype(BF16)
        res = jnp.dot(a_ref[cl], w_c, preferred_element_type=F32)
        carry = pltpu.roll(res[:, CONV_BLK:], 1, 0)
        y = res[:, :CONV_BLK] + jnp.where(first_blk, 0.0, carry) + cb_ref[c]
        for r8 in range(0, rt, SUBLANES):
            y_ref[pl.ds(r8 * CONV_ROW_PITCH + c, SUBLANES, stride=CONV_ROW_PITCH), :] = y[r8:r8 + SUBLANES]

    @pl.when(g == pl.num_programs(1) - 1)
    def _():
        for r in range(rt):
            x = y_ref[r * CONV_ROW_PITCH:r * CONV_ROW_PITCH + CONV_CH, :]
            mu = jnp.mean(x, axis=0, keepdims=True)
            d = x - mu
            var = jnp.mean(d * d, axis=0, keepdims=True)
            z = (d * lax.rsqrt(var + LN_EPS)).T * g_ref[...] + beta_ref[...]
            o_ref[r * CONV_BLK:(r + 1) * CONV_BLK, :] = (z * jax.nn.sigmoid(z)).astype(o_ref.dtype)


def _conv_module(ut, conv_w, conv_b, ln_g, ln_b):
    b, c, s = ut.shape
    rows_per_seq = s // CONV_BLK
    n_rows = b * rows_per_seq
    row_tile = math.gcd(n_rows, CONV_ROW_TILE)
    assert row_tile % rows_per_seq == 0
    a = ut.reshape(b, c, rows_per_seq, CONV_BLK).transpose(1, 0, 2, 3).reshape(c, n_rows, CONV_BLK)
    taps = jnp.pad(conv_w[::-1].T.astype(F32), ((0, 0), (0, 2 * CONV_BLK - CONV_WIDTH)))
    return pl.pallas_call(
        functools.partial(_conv_kernel, rows_per_seq),
        grid=(n_rows // row_tile, c // CONV_CH_TILE),
        in_specs=[pl.BlockSpec(memory_space=pltpu.SMEM),
                  pl.BlockSpec((CONV_CH_TILE, row_tile, CONV_BLK), lambda i, g: (g, i, 0)),
                  pl.BlockSpec((CONV_CH_TILE, 2 * CONV_BLK), lambda i, g: (g, 0)),
                  _resident(ln_g.shape), _resident(ln_b.shape)],
        out_specs=pl.BlockSpec((row_tile * CONV_BLK, c), lambda i, g: (i, 0)),
        out_shape=jax.ShapeDtypeStruct((b * s, c), BF16),
        scratch_shapes=[pltpu.VMEM((row_tile * CONV_ROW_PITCH, CONV_BLK), F32)],
        compiler_params=pltpu.CompilerParams(
            dimension_semantics=("arbitrary", "arbitrary"), vmem_limit_bytes=VMEM_LIMIT),
        name="conv_module",
    )(conv_b.astype(F32), a, taps, ln_g, ln_b)


def _mix_ffn_kernel(alpha, x_ref, att_ref, cv_ref, gate_ref, p_ref,
                    wa_ref, wc_ref, wm_ref, g1_ref, b1_ref,
                    wg_ref, wu_ref, wd_ref, wpg_ref, bpg_ref, wp_ref, g2_ref, b2_ref, o_ref):
    tm, d = x_ref.shape
    for h in range(MIX_ROW_SPLIT):
        rs = slice(h * (tm // MIX_ROW_SPLIT), (h + 1) * (tm // MIX_ROW_SPLIT))
        y_att = jnp.dot(att_ref[rs, :], wa_ref[...], preferred_element_type=F32)
        y_conv = jnp.dot(cv_ref[rs, :], wc_ref[...], preferred_element_type=F32)
        merged = gate_ref[rs, :d].astype(F32) * y_att + gate_ref[rs, d:].astype(F32) * y_conv
        mixed = jnp.dot(merged.astype(BF16), wm_ref[...], preferred_element_type=F32)
        x1 = _layer_norm(alpha * x_ref[rs, :] + mixed, g1_ref[...], b1_ref[...])

        xb = x1.astype(BF16)
        hg = jnp.dot(xb, wg_ref[...], preferred_element_type=F32)
        hu = jnp.dot(xb, wu_ref[...], preferred_element_type=F32)
        hid = (hg * jax.nn.sigmoid(hg) * hu).astype(BF16)
        ffn = jnp.dot(hid, wd_ref[...], preferred_element_type=F32)
        pg = jax.nn.sigmoid(jnp.dot(xb, wpg_ref[...], preferred_element_type=F32) + bpg_ref[...])
        pe = jnp.dot(p_ref[rs, :].astype(BF16), wp_ref[...], preferred_element_type=F32)
        o_ref[rs, :] = _layer_norm(alpha * x1 + ffn + pg * pe, g2_ref[...], b2_ref[...])


def _mix_ffn(alpha, x, att, cv, gates, p, weights, tm):
    t, d = x.shape
    row = lambda n: pl.BlockSpec((tm, n), lambda i: (i, 0))
    return pl.pallas_call(
        functools.partial(_mix_ffn_kernel, alpha),
        grid=(t // tm,),
        in_specs=[row(d), row(ATT_W), row(CONV_CH), row(2 * d), row(PLE_DIM)]
                 + [_resident(w.shape) for w in weights],
        out_specs=row(d),
        out_shape=jax.ShapeDtypeStruct((t, d), F32),
        compiler_params=pltpu.CompilerParams(
            dimension_semantics=("arbitrary",), vmem_limit_bytes=VMEM_LIMIT),
        name="mix_ffn",
    )(x, att, cv, gates, p, *weights)


def kernel(x, p, w_in, b_gate, bias_table, w_att_out, conv_w, conv_b, conv_ln_g, conv_ln_b,
           w_conv_out, w_mix_out, ln_mix_g, ln_mix_b, w_ffn_gate, w_ffn_up, w_ffn_down,
           w_ple, w_ple_gate, b_ple_gate, ln_ffn_g, ln_ffn_b):
    b, s, d = x.shape
    depth = w_in.shape[0]
    assert d == D_MODEL and s % MOBA_BLOCK == 0
    alpha = (2.0 * depth) ** 0.25
    tm = 1024
    tm_ffn = 512
    row = lambda v: v.reshape(1, -1).astype(F32)
    k_end, v_end = 2 * ATT_W, 3 * ATT_W
    u_end = v_end + 2 * CONV_CH

    tbl_t = bias_table.T.astype(F32)
    bias_own, bias_prev = _bias_tables(tbl_t)
    for i in range(depth):
        w = w_in[i].astype(BF16)
        qk, vt, ut, gates = _proj(x, w[:, :k_end], w[:, k_end:v_end].T, w[:, v_end:u_end].T,
                                  w[:, u_end:], row(b_gate[i]), tm)
        att = _attention(tbl_t, qk, vt, bias_own, bias_prev)
        cv = _conv_module(ut, conv_w[i], conv_b[i], row(conv_ln_g[i]), row(conv_ln_b[i]))
        weights = (w_att_out[i].astype(BF16), w_conv_out[i].astype(BF16), w_mix_out[i].astype(BF16),
                   row(ln_mix_g[i]), row(ln_mix_b[i]),
                   w_ffn_gate[i].astype(BF16), w_ffn_up[i].astype(BF16), w_ffn_down[i].astype(BF16),
                   w_ple_gate[i].astype(BF16), row(b_ple_gate[i]), w_ple[i].astype(BF16),
                   row(ln_ffn_g[i]), row(ln_ffn_b[i]))
        x2 = _mix_ffn(alpha, x.reshape(b * s, d), att.reshape(b * s, ATT_W), cv,
                      gates.reshape(b * s, 2 * d), p[i].reshape(b * s, PLE_DIM), weights, tm_ffn)
        x = x2.reshape(b, s, d)
    return x
```

```python
import functools
import math

import numpy as np
import jax
import jax.numpy as jnp
from jax import lax
from jax.experimental import pallas as pl
from jax.experimental.pallas import tpu as pltpu

D_MODEL = 1024
PLE_DIM = 256
ATT_HEADS = 8
HEAD_DIM = 64
ATT_W = ATT_HEADS * HEAD_DIM
MOBA_BLOCK = 256
MOBA_TOPK = 3
CONV_CH = 512
CONV_WIDTH = 31
NUM_BUCKETS = 32
MAX_DISTANCE = 128
FFN_HIDDEN = 2816
LN_EPS = 1e-5
NEG_INF = -1e30

LANES = 128
SUBLANES = 8
HEADS_PER_STEP = LANES // HEAD_DIM
ONES_ROWS = 16
LOG2E = math.log2(math.e)
ATTN_SLOTS = 4
ATTN_ROWS = 2
MIX_ROW_SPLIT = 2
CONV_BLK = LANES
CONV_ROW_TILE = 64
CONV_CH_TILE = 64
CONV_ROW_PITCH = CONV_CH + 8
VMEM_LIMIT = 56 * 1024 * 1024

BF16 = jnp.bfloat16
F32 = jnp.float32
NT_DIMS = (((1,), (1,)), ((), ()))


def _resident(shape):
    return pl.BlockSpec(shape, lambda *_: (0,) * len(shape), pipeline_mode=pl.Buffered(1))


def _layer_norm(h, g, b):
    mu = jnp.mean(h, axis=-1, keepdims=True)
    d = h - mu
    var = jnp.mean(d * d, axis=-1, keepdims=True)
    return d * lax.rsqrt(var + LN_EPS) * g + b


def _t5_bucket_np(rel):
    n = np.maximum(rel, 0)
    max_exact = NUM_BUCKETS // 2
    nf = np.maximum(n, 1).astype(np.float32)
    large = max_exact + (np.log(nf / np.float32(max_exact)) / np.float32(math.log(MAX_DISTANCE / max_exact))
                         * np.float32(NUM_BUCKETS - max_exact)).astype(np.int32)
    large = np.minimum(large, NUM_BUCKETS - 1)
    return np.where(n < max_exact, n, large).astype(np.int32)


def _bias_bucket_tables():
    kpos = np.arange(MOBA_BLOCK)[:, None]
    qpos = np.arange(MOBA_BLOCK)[None, :]
    rel_own = qpos - kpos
    own = np.where(rel_own >= 0, _t5_bucket_np(rel_own), NUM_BUCKETS)
    prev = _t5_bucket_np(rel_own + MOBA_BLOCK)
    return own.astype(np.int32), prev.astype(np.int32)


def _bias_kernel(tbl_ref, own_ix_ref, prev_ix_ref, own_ref, prev_ref):
    h = pl.program_id(0)
    own_ix = own_ix_ref[...]
    prev_ix = prev_ix_ref[...]
    own = jnp.full(own_ix.shape, NEG_INF, F32)
    prev = jnp.zeros(prev_ix.shape, F32)
    for b in range(NUM_BUCKETS):
        t = tbl_ref[h, b] * LOG2E
        own = jnp.where(own_ix == b, t, own)
        prev = jnp.where(prev_ix == b, t, prev)
    own_ref[0] = own
    prev_ref[0] = prev


def _bias_tables(tbl_t):
    own_ix, prev_ix = _bias_bucket_tables()
    blk = (MOBA_BLOCK, MOBA_BLOCK)
    return pl.pallas_call(
        _bias_kernel,
        grid=(ATT_HEADS,),
        in_specs=[pl.BlockSpec(memory_space=pltpu.SMEM),
                  pl.BlockSpec(blk, lambda h: (0, 0)),
                  pl.BlockSpec(blk, lambda h: (0, 0))],
        out_specs=[pl.BlockSpec((1,) + blk, lambda h: (h, 0, 0)),
                   pl.BlockSpec((1,) + blk, lambda h: (h, 0, 0))],
        out_shape=[jax.ShapeDtypeStruct((ATT_HEADS,) + blk, F32)] * 2,
        name="t5_bias",
    )(tbl_t, jnp.asarray(own_ix), jnp.asarray(prev_ix))


def _proj_kernel(x_ref, wqk_ref, wvt_ref, wagt_ref, wgate_ref, bgate_ref,
                 qk_ref, vt_ref, ut_ref, gate_ref):
    xb = x_ref[0].astype(BF16)
    gl = jnp.dot(xb, wgate_ref[...], preferred_element_type=F32) + bgate_ref[...]
    gate_ref[0] = jax.nn.sigmoid(gl).astype(BF16)
    agt = lax.dot_general(wagt_ref[...], xb, NT_DIMS, preferred_element_type=F32)
    ut_ref[0] = (agt[:CONV_CH] * jax.nn.sigmoid(agt[CONV_CH:])).astype(BF16)
    qk = jnp.dot(xb, wqk_ref[...], preferred_element_type=F32)
    qk_ref[0, :, :ATT_W] = (qk[:, :ATT_W] * (LOG2E * HEAD_DIM ** -0.5)).astype(BF16)
    qk_ref[0, :, ATT_W:] = qk[:, ATT_W:].astype(BF16)
    vt = lax.dot_general(wvt_ref[...], xb, NT_DIMS, preferred_element_type=F32)
    vt_ref[0] = vt.astype(BF16)


def _proj(x, wqk, wvt, wagt, wgate, bgate, tm):
    b, s, d = x.shape
    return pl.pallas_call(
        _proj_kernel,
        grid=(b, s // tm),
        in_specs=[pl.BlockSpec((1, tm, d), lambda i, j: (i, j, 0)),
                  _resident(wqk.shape), _resident(wvt.shape), _resident(wagt.shape),
                  _resident(wgate.shape), _resident(bgate.shape)],
        out_specs=[pl.BlockSpec((1, tm, 2 * ATT_W), lambda i, j: (i, j, 0)),
                   pl.BlockSpec((1, ATT_W, tm), lambda i, j: (i, 0, j)),
                   pl.BlockSpec((1, CONV_CH, tm), lambda i, j: (i, 0, j)),
                   pl.BlockSpec((1, tm, 2 * d), lambda i, j: (i, j, 0))],
        out_shape=[jax.ShapeDtypeStruct((b, s, 2 * ATT_W), BF16),
                   jax.ShapeDtypeStruct((b, ATT_W, s), BF16),
                   jax.ShapeDtypeStruct((b, CONV_CH, s), BF16),
                   jax.ShapeDtypeStruct((b, s, 2 * d), BF16)],
        compiler_params=pltpu.CompilerParams(
            dimension_semantics=("arbitrary", "arbitrary"), vmem_limit_bytes=VMEM_LIMIT),
        name="in_proj",
    )(x, wqk, wvt, wagt, wgate, bgate)


def _attn_kernel(tbl_ref, q_ref, k_ref, vt_ref, bown_ref, bprev_ref, o_ref,
                 qm_ref, vaug_ref, *sp_refs):
    s_refs, p_refs = sp_refs[:ATTN_SLOTS], sp_refs[ATTN_SLOTS:]
    hp = pl.program_id(0)
    n_rows, s, _ = q_ref.shape
    nb = s // MOBA_BLOCK
    lane = lax.broadcasted_iota(jnp.int32, (1, LANES), 1)
    q_blk = lax.broadcasted_iota(jnp.int32, (1, s), 1) // MOBA_BLOCK
    blk = lax.broadcasted_iota(jnp.int32, (nb, 1), 0)
    past = blk < q_blk
    far_bias = [tbl_ref[hp * HEADS_PER_STEP + hh, NUM_BUCKETS - 1] * LOG2E for hh in range(HEADS_PER_STEP)]

    sel_neg = {}
    for r in range(n_rows):
        q = q_ref[r]
        k = k_ref[r]
        k_mean = (jnp.sum(k.astype(F32).reshape(nb, MOBA_BLOCK, LANES), axis=1)
                  * (1.0 / MOBA_BLOCK)).astype(BF16)
        for hh in range(HEADS_PER_STEP):
            rh = r * HEADS_PER_STEP + hh
            qm = jnp.where(lane // HEAD_DIM == hh, q, jnp.zeros_like(q))
            qm_ref[rh] = qm
            vaug_ref[rh, :HEAD_DIM, :] = vt_ref[r, hh * HEAD_DIM:(hh + 1) * HEAD_DIM, :]
            vaug_ref[rh, HEAD_DIM:, :] = jnp.ones((ONES_ROWS, s), BF16)
            gate = lax.dot_general(k_mean, qm, NT_DIMS, preferred_element_type=F32)
            rank = jnp.zeros((nb, s), jnp.int32)
            for m in range(nb):
                gm = gate[m:m + 1, :]
                beats = (gm > gate) | ((gm == gate) & (m < blk))
                rank = rank + jnp.where(beats & (m < q_blk), 1, 0)
            sel_neg[r, hh] = jnp.where(past & (rank < MOBA_TOPK), 0.0, NEG_INF).astype(F32)

    rows = lambda j: slice(j * MOBA_BLOCK, (j + 1) * MOBA_BLOCK)
    chains = [(i, r, hh) for i in range(nb) for r in range(n_rows) for hh in range(HEADS_PER_STEP)]
    slot_of = {chain: c % ATTN_SLOTS for c, chain in enumerate(chains)}
    shifts = {}
    outs = {}

    def pass_a(i, r, hh):
        hi = (i + 1) * MOBA_BLOCK
        slot = slot_of[i, r, hh]
        st = lax.dot_general(k_ref[r, :hi, :], qm_ref[r * HEADS_PER_STEP + hh, rows(i), :], NT_DIMS,
                             preferred_element_type=F32)
        col_max, offs = None, []
        for j in range(i + 1):
            blk_t = st[rows(j)]
            if j == i:
                blk_t = blk_t + bown_ref[hh]
                off = None
            elif j == i - 1:
                blk_t = blk_t + bprev_ref[hh]
                off = sel_neg[r, hh][j:j + 1, rows(i)]
            else:
                off = sel_neg[r, hh][j:j + 1, rows(i)] + far_bias[hh]
            s_refs[slot][rows(j), :] = blk_t
            cand = jnp.max(blk_t, axis=0, keepdims=True)
            cand = cand if off is None else cand + off
            col_max = cand if col_max is None else jnp.maximum(col_max, cand)
            offs.append(off)
        shifts[i, r, hh] = [col_max if off is None else col_max - off for off in offs]

    def pass_b(i, r, hh):
        hi = (i + 1) * MOBA_BLOCK
        slot = slot_of[i, r, hh]
        shift = shifts.pop((i, r, hh))
        for j in range(i + 1):
            p_refs[slot][rows(j), :] = jnp.exp2(s_refs[slot][rows(j), :] - shift[j]).astype(BF16)
        o_aug = jnp.dot(vaug_ref[r * HEADS_PER_STEP + hh, :, :hi], p_refs[slot][:hi, :],
                        preferred_element_type=F32)
        outs[hh] = o_aug[:HEAD_DIM] / o_aug[HEAD_DIM:HEAD_DIM + 1]
        if hh == HEADS_PER_STEP - 1:
            o_pair = [outs.pop(h2) for h2 in range(HEADS_PER_STEP)]
            o_ref[r, rows(i), :] = jnp.concatenate(o_pair, axis=0).T.astype(o_ref.dtype)

    for c in range(len(chains) + ATTN_SLOTS - 1):
        if c < len(chains):
            pass_a(*chains[c])
        if c >= ATTN_SLOTS - 1:
            pass_b(*chains[c - (ATTN_SLOTS - 1)])


def _attention(tbl_t, qk, vt, bias_own, bias_prev):
    b, s, _ = qk.shape
    n_groups = ATT_W // LANES
    n_rows = math.gcd(b, ATTN_ROWS)
    blk = (HEADS_PER_STEP, MOBA_BLOCK, MOBA_BLOCK)
    return pl.pallas_call(
        _attn_kernel,
        grid=(n_groups, b // n_rows),
        in_specs=[pl.BlockSpec(memory_space=pltpu.SMEM),
                  pl.BlockSpec((n_rows, s, LANES), lambda g, i: (i, 0, g)),
                  pl.BlockSpec((n_rows, s, LANES), lambda g, i: (i, 0, n_groups + g)),
                  pl.BlockSpec((n_rows, LANES, s), lambda g, i: (i, g, 0)),
                  pl.BlockSpec(blk, lambda g, i: (g, 0, 0)),
                  pl.BlockSpec(blk, lambda g, i: (g, 0, 0))],
        out_specs=pl.BlockSpec((n_rows, s, LANES), lambda g, i: (i, 0, g)),
        out_shape=jax.ShapeDtypeStruct((b, s, ATT_W), BF16),
        scratch_shapes=[pltpu.VMEM((n_rows * HEADS_PER_STEP, s, LANES), BF16),
                        pltpu.VMEM((n_rows * HEADS_PER_STEP, HEAD_DIM + ONES_ROWS, s), BF16),
                        *[pltpu.VMEM((s, MOBA_BLOCK), F32)] * ATTN_SLOTS,
                        *[pltpu.VMEM((s, MOBA_BLOCK), BF16)] * ATTN_SLOTS],
        compiler_params=pltpu.CompilerParams(
            dimension_semantics=("arbitrary", "arbitrary"), vmem_limit_bytes=VMEM_LIMIT),
        name="moba_attn",
    )(tbl_t, qk, qk, vt, bias_own, bias_prev)


def _conv_kernel(rows_per_seq, cb_ref, a_ref, taps_ref, g_ref, beta_ref, o_ref, y_ref):
    g = pl.program_id(1)
    ct, rt, _ = a_ref.shape
    first_blk = lax.broadcasted_iota(jnp.int32, (rt, 1), 0) % rows_per_seq == 0
    for cl in range(ct):
        c = g * ct + cl
        taps = jnp.broadcast_to(taps_ref[cl:cl + 1, :], (CONV_BLK, 2 * CONV_BLK))
        w_c = pltpu.roll(taps, 0, 1, stride=1, stride_axis=0).astype(BF16)
        res = jnp.dot(a_ref[cl], w_c, preferred_element_type=F32)
        carry = pltpu.roll(res[:, CONV_BLK:], 1, 0)
        y = res[:, :CONV_BLK] + jnp.where(first_blk, 0.0, carry) + cb_ref[c]
        for r8 in range(0, rt, SUBLANES):
            y_ref[pl.ds(r8 * CONV_ROW_PITCH + c, SUBLANES, stride=CONV_ROW_PITCH), :] = y[r8:r8 + SUBLANES]

    @pl.when(g == pl.num_programs(1) - 1)
    def _():
        for r in range(rt):
            x = y_ref[r * CONV_ROW_PITCH:r * CONV_ROW_PITCH + CONV_CH, :]
            mu = jnp.mean(x, axis=0, keepdims=True)
            d = x - mu
            var = jnp.mean(d * d, axis=0, keepdims=True)
            z = (d * lax.rsqrt(var + LN_EPS)).T * g_ref[...] + beta_ref[...]
            o_ref[r * CONV_BLK:(r + 1) * CONV_BLK, :] = (z * jax.nn.sigmoid(z)).astype(o_ref.dtype)


def _conv_module(ut, conv_w, conv_b, ln_g, ln_b):
    b, c, s = ut.shape
    rows_per_seq = s // CONV_BLK
    n_rows = b * rows_per_seq
    row_tile = math.gcd(n_rows, CONV_ROW_TILE)
    assert row_tile % rows_per_seq == 0
    a = ut.reshape(b, c, rows_per_seq, CONV_BLK).transpose(1, 0, 2, 3).reshape(c, n_rows, CONV_BLK)
    taps = jnp.pad(conv_w[::-1].T.astype(F32), ((0, 0), (0, 2 * CONV_BLK - CONV_WIDTH)))
    return pl.pallas_call(
        functools.partial(_conv_kernel, rows_per_seq),
        grid=(n_rows // row_tile, c // CONV_CH_TILE),
        in_specs=[pl.BlockSpec(memory_space=pltpu.SMEM),
                  pl.BlockSpec((CONV_CH_TILE, row_tile, CONV_BLK), lambda i, g: (g, i, 0)),
                  pl.BlockSpec((CONV_CH_TILE, 2 * CONV_BLK), lambda i, g: (g, 0)),
                  _resident(ln_g.shape), _resident(ln_b.shape)],
        out_specs=pl.BlockSpec((row_tile * CONV_BLK, c), lambda i, g: (i, 0)),
        out_shape=jax.ShapeDtypeStruct((b * s, c), BF16),
        scratch_shapes=[pltpu.VMEM((row_tile * CONV_ROW_PITCH, CONV_BLK), F32)],
        compiler_params=pltpu.CompilerParams(
            dimension_semantics=("arbitrary", "arbitrary"), vmem_limit_bytes=VMEM_LIMIT),
        name="conv_module",
    )(conv_b.astype(F32), a, taps, ln_g, ln_b)


def _mix_ffn_kernel(alpha, x_ref, att_ref, cv_ref, gate_ref, p_ref,
                    wa_ref, wc_ref, wm_ref, g1_ref, b1_ref,
                    wg_ref, wu_ref, wd_ref, wpg_ref, bpg_ref, wp_ref, g2_ref, b2_ref, o_ref):
    tm, d = x_ref.shape
    groups = [slice(h * (tm // MIX_ROW_SPLIT), (h + 1) * (tm // MIX_ROW_SPLIT)) for h in range(MIX_ROW_SPLIT)]
    dot = functools.partial(jnp.dot, preferred_element_type=F32)
    y_att = [dot(att_ref[rs, :], wa_ref[...]) for rs in groups]
    y_conv = [dot(cv_ref[rs, :], wc_ref[...]) for rs in groups]
    merged = [(gate_ref[rs, :d].astype(F32) * ya + gate_ref[rs, d:].astype(F32) * yc).astype(BF16)
              for rs, ya, yc in zip(groups, y_att, y_conv)]
    mixed = [dot(m, wm_ref[...]) for m in merged]
    pe = [dot(p_ref[rs, :].astype(BF16), wp_ref[...]) for rs in groups]
    x1 = [_layer_norm(alpha * x_ref[rs, :] + mx, g1_ref[...], b1_ref[...]) for rs, mx in zip(groups, mixed)]
    xb = [v.astype(BF16) for v in x1]
    hg = [dot(v, wg_ref[...]) for v in xb]
    hu = [dot(v, wu_ref[...]) for v in xb]
    pg = [dot(v, wpg_ref[...]) for v in xb]
    hid = [(g * jax.nn.sigmoid(g) * u).astype(BF16) for g, u in zip(hg, hu)]
    ffn = [dot(v, wd_ref[...]) for v in hid]
    for h, rs in enumerate(groups):
        ple = jax.nn.sigmoid(pg[h] + bpg_ref[...]) * pe[h]
        o_ref[rs, :] = _layer_norm(alpha * x1[h] + ffn[h] + ple, g2_ref[...], b2_ref[...])


def _mix_ffn(alpha, x, att, cv, gates, p, weights, tm):
    t, d = x.shape
    row = lambda n: pl.BlockSpec((tm, n), lambda i: (i, 0))
    return pl.pallas_call(
        functools.partial(_mix_ffn_kernel, alpha),
        grid=(t // tm,),
        in_specs=[row(d), row(ATT_W), row(CONV_CH), row(2 * d), row(PLE_DIM)]
                 + [_resident(w.shape) for w in weights],
        out_specs=row(d),
        out_shape=jax.ShapeDtypeStruct((t, d), F32),
        compiler_params=pltpu.CompilerParams(
            dimension_semantics=("arbitrary",), vmem_limit_bytes=VMEM_LIMIT),
        name="mix_ffn",
    )(x, att, cv, gates, p, *weights)


def kernel(x, p, w_in, b_gate, bias_table, w_att_out, conv_w, conv_b, conv_ln_g, conv_ln_b,
           w_conv_out, w_mix_out, ln_mix_g, ln_mix_b, w_ffn_gate, w_ffn_up, w_ffn_down,
           w_ple, w_ple_gate, b_ple_gate, ln_ffn_g, ln_ffn_b):
    b, s, d = x.shape
    depth = w_in.shape[0]
    assert d == D_MODEL and s % MOBA_BLOCK == 0
    alpha = (2.0 * depth) ** 0.25
    tm = 1024
    tm_ffn = 512
    row = lambda v: v.reshape(1, -1).astype(F32)
    k_end, v_end = 2 * ATT_W, 3 * ATT_W
    u_end = v_end + 2 * CONV_CH

    tbl_t = bias_table.T.astype(F32)
    bias_own, bias_prev = _bias_tables(tbl_t)
    for i in range(depth):
        w = w_in[i].astype(BF16)
        qk, vt, ut, gates = _proj(x, w[:, :k_end], w[:, k_end:v_end].T, w[:, v_end:u_end].T,
                                  w[:, u_end:], row(b_gate[i]), tm)
        att = _attention(tbl_t, qk, vt, bias_own, bias_prev)
        cv = _conv_module(ut, conv_w[i], conv_b[i], row(conv_ln_g[i]), row(conv_ln_b[i]))
        weights = (w_att_out[i].astype(BF16), w_conv_out[i].astype(BF16), w_mix_out[i].astype(BF16),
                   row(ln_mix_g[i]), row(ln_mix_b[i]),
                   w_ffn_gate[i].astype(BF16), w_ffn_up[i].astype(BF16), w_ffn_down[i].astype(BF16),
                   w_ple_gate[i].astype(BF16), row(b_ple_gate[i]), w_ple[i].astype(BF16),
                   row(ln_ffn_g[i]), row(ln_ffn_b[i]))
        x2 = _mix_ffn(alpha, x.reshape(b * s, d), att.reshape(b * s, ATT_W), cv,
                      gates.reshape(b * s, 2 * d), p[i].reshape(b * s, PLE_DIM), weights, tm_ffn)
        x = x2.reshape(b, s, d)
    return x
```

```python
import functools
import math

import numpy as np
import jax
import jax.numpy as jnp
from jax import lax
from jax.experimental import pallas as pl
from jax.experimental.pallas import tpu as pltpu

D_MODEL = 1024
PLE_DIM = 256
ATT_HEADS = 8
HEAD_DIM = 64
ATT_W = ATT_HEADS * HEAD_DIM
MOBA_BLOCK = 256
MOBA_TOPK = 3
CONV_CH = 512
CONV_WIDTH = 31
NUM_BUCKETS = 32
MAX_DISTANCE = 128
FFN_HIDDEN = 2816
LN_EPS = 1e-5
NEG_INF = -1e30

LANES = 128
SUBLANES = 8
HEADS_PER_STEP = LANES // HEAD_DIM
ONES_ROWS = 16
LOG2E = math.log2(math.e)
ATTN_SLOTS = 4
ATTN_ROWS = 2
PROJ_ROW_SPLIT = 1
MIX_ROW_FRACTIONS = (1, 1)
CONV_BLK = LANES
CONV_ROW_TILE = 64
CONV_CH_TILE = 64
CONV_ROW_PITCH = CONV_CH + 8
VMEM_LIMIT = 56 * 1024 * 1024

BF16 = jnp.bfloat16
F32 = jnp.float32
NT_DIMS = (((1,), (1,)), ((), ()))


def _resident(shape):
    return pl.BlockSpec(shape, lambda *_: (0,) * len(shape), pipeline_mode=pl.Buffered(1))


def _layer_norm(h, g, b):
    mu = jnp.mean(h, axis=-1, keepdims=True)
    d = h - mu
    var = jnp.mean(d * d, axis=-1, keepdims=True)
    return d * lax.rsqrt(var + LN_EPS) * g + b


def _t5_bucket_np(rel):
    n = np.maximum(rel, 0)
    max_exact = NUM_BUCKETS // 2
    nf = np.maximum(n, 1).astype(np.float32)
    large = max_exact + (np.log(nf / np.float32(max_exact)) / np.float32(math.log(MAX_DISTANCE / max_exact))
                         * np.float32(NUM_BUCKETS - max_exact)).astype(np.int32)
    large = np.minimum(large, NUM_BUCKETS - 1)
    return np.where(n < max_exact, n, large).astype(np.int32)


def _bias_bucket_tables():
    kpos = np.arange(MOBA_BLOCK)[:, None]
    qpos = np.arange(MOBA_BLOCK)[None, :]
    rel_own = qpos - kpos
    own = np.where(rel_own >= 0, _t5_bucket_np(rel_own), NUM_BUCKETS)
    prev = _t5_bucket_np(rel_own + MOBA_BLOCK)
    return own.astype(np.int32), prev.astype(np.int32)


def _bias_kernel(tbl_ref, own_ix_ref, prev_ix_ref, own_ref, prev_ref):
    h = pl.program_id(0)
    own_ix = own_ix_ref[...]
    prev_ix = prev_ix_ref[...]
    own = jnp.full(own_ix.shape, NEG_INF, F32)
    prev = jnp.zeros(prev_ix.shape, F32)
    for b in range(NUM_BUCKETS):
        t = tbl_ref[h, b] * LOG2E
        own = jnp.where(own_ix == b, t, own)
        prev = jnp.where(prev_ix == b, t, prev)
    own_ref[0] = own
    prev_ref[0] = prev


def _bias_tables(tbl_t):
    own_ix, prev_ix = _bias_bucket_tables()
    blk = (MOBA_BLOCK, MOBA_BLOCK)
    return pl.pallas_call(
        _bias_kernel,
        grid=(ATT_HEADS,),
        in_specs=[pl.BlockSpec(memory_space=pltpu.SMEM),
                  pl.BlockSpec(blk, lambda h: (0, 0)),
                  pl.BlockSpec(blk, lambda h: (0, 0))],
        out_specs=[pl.BlockSpec((1,) + blk, lambda h: (h, 0, 0)),
                   pl.BlockSpec((1,) + blk, lambda h: (h, 0, 0))],
        out_shape=[jax.ShapeDtypeStruct((ATT_HEADS,) + blk, F32)] * 2,
        name="t5_bias",
    )(tbl_t, jnp.asarray(own_ix), jnp.asarray(prev_ix))


def _proj_kernel(x_ref, wqk_ref, wvt_ref, wagt_ref, wgate_ref, bgate_ref,
                 qk_ref, vt_ref, ut_ref, gate_ref):
    tm = x_ref.shape[1]
    groups = [slice(h * (tm // PROJ_ROW_SPLIT), (h + 1) * (tm // PROJ_ROW_SPLIT)) for h in range(PROJ_ROW_SPLIT)]
    xb = [x_ref[0, rs, :].astype(BF16) for rs in groups]
    for rs, v in zip(groups, xb):
        gl = jnp.dot(v, wgate_ref[...], preferred_element_type=F32) + bgate_ref[...]
        gate_ref[0, rs, :] = jax.nn.sigmoid(gl).astype(BF16)
    for rs, v in zip(groups, xb):
        agt = lax.dot_general(wagt_ref[...], v, NT_DIMS, preferred_element_type=F32)
        ut_ref[0, :, rs] = (agt[:CONV_CH] * jax.nn.sigmoid(agt[CONV_CH:])).astype(BF16)
    for rs, v in zip(groups, xb):
        qk = jnp.dot(v, wqk_ref[...], preferred_element_type=F32)
        qk_ref[0, rs, :ATT_W] = (qk[:, :ATT_W] * (LOG2E * HEAD_DIM ** -0.5)).astype(BF16)
        qk_ref[0, rs, ATT_W:] = qk[:, ATT_W:].astype(BF16)
    for rs, v in zip(groups, xb):
        vt = lax.dot_general(wvt_ref[...], v, NT_DIMS, preferred_element_type=F32)
        vt_ref[0, :, rs] = vt.astype(BF16)


def _proj(x, wqk, wvt, wagt, wgate, bgate, tm):
    b, s, d = x.shape
    return pl.pallas_call(
        _proj_kernel,
        grid=(b, s // tm),
        in_specs=[pl.BlockSpec((1, tm, d), lambda i, j: (i, j, 0)),
                  _resident(wqk.shape), _resident(wvt.shape), _resident(wagt.shape),
                  _resident(wgate.shape), _resident(bgate.shape)],
        out_specs=[pl.BlockSpec((1, tm, 2 * ATT_W), lambda i, j: (i, j, 0)),
                   pl.BlockSpec((1, ATT_W, tm), lambda i, j: (i, 0, j)),
                   pl.BlockSpec((1, CONV_CH, tm), lambda i, j: (i, 0, j)),
                   pl.BlockSpec((1, tm, 2 * d), lambda i, j: (i, j, 0))],
        out_shape=[jax.ShapeDtypeStruct((b, s, 2 * ATT_W), BF16),
                   jax.ShapeDtypeStruct((b, ATT_W, s), BF16),
                   jax.ShapeDtypeStruct((b, CONV_CH, s), BF16),
                   jax.ShapeDtypeStruct((b, s, 2 * d), BF16)],
        compiler_params=pltpu.CompilerParams(
            dimension_semantics=("arbitrary", "arbitrary"), vmem_limit_bytes=VMEM_LIMIT),
        name="in_proj",
    )(x, wqk, wvt, wagt, wgate, bgate)


def _attn_kernel(tbl_ref, q_ref, k_ref, vt_ref, bown_ref, bprev_ref, o_ref,
                 qm_ref, vaug_ref, sel_ref, shift_ref, *sp_refs):
    s_refs, p_refs = sp_refs[:ATTN_SLOTS], sp_refs[ATTN_SLOTS:]
    hp = pl.program_id(0)
    n_rows, s, _ = q_ref.shape
    nb = s // MOBA_BLOCK
    lane = lax.broadcasted_iota(jnp.int32, (1, LANES), 1)
    q_blk = lax.broadcasted_iota(jnp.int32, (1, s), 1) // MOBA_BLOCK
    blk = lax.broadcasted_iota(jnp.int32, (nb, 1), 0)
    past = blk < q_blk
    far_bias = [tbl_ref[hp * HEADS_PER_STEP + hh, NUM_BUCKETS - 1] * LOG2E for hh in range(HEADS_PER_STEP)]

    for r in range(n_rows):
        q = q_ref[r]
        k = k_ref[r]
        k_mean = (jnp.sum(k.astype(F32).reshape(nb, MOBA_BLOCK, LANES), axis=1)
                  * (1.0 / MOBA_BLOCK)).astype(BF16)
        for hh in range(HEADS_PER_STEP):
            rh = r * HEADS_PER_STEP + hh
            qm = jnp.where(lane // HEAD_DIM == hh, q, jnp.zeros_like(q))
            qm_ref[rh] = qm
            vaug_ref[rh, :HEAD_DIM, :] = vt_ref[r, hh * HEAD_DIM:(hh + 1) * HEAD_DIM, :]
            vaug_ref[rh, HEAD_DIM:, :] = jnp.ones((ONES_ROWS, s), BF16)
            gate = lax.dot_general(k_mean, qm, NT_DIMS, preferred_element_type=F32)
            rank = jnp.zeros((nb, s), jnp.int32)
            for m in range(nb):
                gm = gate[m:m + 1, :]
                beats = (gm > gate) | ((gm == gate) & (m < blk))
                rank = rank + jnp.where(beats & (m < q_blk), 1, 0)
            sel_ref[rh] = jnp.where(past & (rank < MOBA_TOPK), 0.0, NEG_INF).astype(F32)

    rows = lambda j: slice(j * MOBA_BLOCK, (j + 1) * MOBA_BLOCK)
    chains = [(i, r, hh) for i in range(nb) for r in range(n_rows) for hh in range(HEADS_PER_STEP)]
    slot_of = {chain: c % ATTN_SLOTS for c, chain in enumerate(chains)}
    outs = {}

    def pass_a(i, r, hh):
        hi = (i + 1) * MOBA_BLOCK
        slot = slot_of[i, r, hh]
        rh = r * HEADS_PER_STEP + hh
        st = lax.dot_general(k_ref[r, :hi, :], qm_ref[r * HEADS_PER_STEP + hh, rows(i), :], NT_DIMS,
                             preferred_element_type=F32)
        col_max, offs = None, []
        for j in range(i + 1):
            blk_t = st[rows(j)]
            if j == i:
                blk_t = blk_t + bown_ref[hh]
                off = None
            elif j == i - 1:
                blk_t = blk_t + bprev_ref[hh]
                off = sel_ref[rh, j:j + 1, rows(i)]
            else:
                off = sel_ref[rh, j:j + 1, rows(i)] + far_bias[hh]
            s_refs[slot][rows(j), :] = blk_t
            cand = jnp.max(blk_t, axis=0, keepdims=True)
            cand = cand if off is None else cand + off
            col_max = cand if col_max is None else jnp.maximum(col_max, cand)
            offs.append(off)
        for j, off in enumerate(offs):
            shift_ref[slot, j:j + 1, :] = col_max if off is None else col_max - off

    def pass_b(i, r, hh):
        hi = (i + 1) * MOBA_BLOCK
        slot = slot_of[i, r, hh]
        for j in range(i + 1):
            p_refs[slot][rows(j), :] = jnp.exp2(s_refs[slot][rows(j), :] - shift_ref[slot, j:j + 1, :]).astype(BF16)
        o_aug = jnp.dot(vaug_ref[r * HEADS_PER_STEP + hh, :, :hi], p_refs[slot][:hi, :],
                        preferred_element_type=F32)
        outs[hh] = o_aug[:HEAD_DIM] / o_aug[HEAD_DIM:HEAD_DIM + 1]
        if hh == HEADS_PER_STEP - 1:
            o_pair = [outs.pop(h2) for h2 in range(HEADS_PER_STEP)]
            o_ref[r, rows(i), :] = jnp.concatenate(o_pair, axis=0).T.astype(o_ref.dtype)

    for c in range(len(chains) + ATTN_SLOTS - 1):
        if c < len(chains):
            pass_a(*chains[c])
        if c >= ATTN_SLOTS - 1:
            pass_b(*chains[c - (ATTN_SLOTS - 1)])


def _attention(tbl_t, qk, vt, bias_own, bias_prev):
    b, s, _ = qk.shape
    n_groups = ATT_W // LANES
    n_rows = math.gcd(b, ATTN_ROWS)
    blk = (HEADS_PER_STEP, MOBA_BLOCK, MOBA_BLOCK)
    return pl.pallas_call(
        _attn_kernel,
        grid=(n_groups, b // n_rows),
        in_specs=[pl.BlockSpec(memory_space=pltpu.SMEM),
                  pl.BlockSpec((n_rows, s, LANES), lambda g, i: (i, 0, g)),
                  pl.BlockSpec((n_rows, s, LANES), lambda g, i: (i, 0, n_groups + g)),
                  pl.BlockSpec((n_rows, LANES, s), lambda g, i: (i, g, 0)),
                  pl.BlockSpec(blk, lambda g, i: (g, 0, 0)),
                  pl.BlockSpec(blk, lambda g, i: (g, 0, 0))],
        out_specs=pl.BlockSpec((n_rows, s, LANES), lambda g, i: (i, 0, g)),
        out_shape=jax.ShapeDtypeStruct((b, s, ATT_W), BF16),
        scratch_shapes=[pltpu.VMEM((n_rows * HEADS_PER_STEP, s, LANES), BF16),
                        pltpu.VMEM((n_rows * HEADS_PER_STEP, HEAD_DIM + ONES_ROWS, s), BF16),
                        pltpu.VMEM((n_rows * HEADS_PER_STEP, s // MOBA_BLOCK, s), F32),
                        pltpu.VMEM((ATTN_SLOTS, s // MOBA_BLOCK, MOBA_BLOCK), F32),
                        *[pltpu.VMEM((s, MOBA_BLOCK), F32)] * ATTN_SLOTS,
                        *[pltpu.VMEM((s, MOBA_BLOCK), BF16)] * ATTN_SLOTS],
        compiler_params=pltpu.CompilerParams(
            dimension_semantics=("arbitrary", "arbitrary"), vmem_limit_bytes=VMEM_LIMIT),
        name="moba_attn",
    )(tbl_t, qk, qk, vt, bias_own, bias_prev)


def _conv_kernel(rows_per_seq, cb_ref, a_ref, taps_ref, o_ref, y_ref):
    g = pl.program_id(1)
    ct, rt, _ = a_ref.shape
    first_blk = lax.broadcasted_iota(jnp.int32, (rt, 1), 0) % rows_per_seq == 0
    for cl in range(ct):
        c = g * ct + cl
        taps = jnp.broadcast_to(taps_ref[cl:cl + 1, :], (CONV_BLK, 2 * CONV_BLK))
        w_c = pltpu.roll(taps, 0, 1, stride=1, stride_axis=0).astype(BF16)
        res = jnp.dot(a_ref[cl], w_c, preferred_element_type=F32)
        carry = pltpu.roll(res[:, CONV_BLK:], 1, 0)
        y = res[:, :CONV_BLK] + jnp.where(first_blk, 0.0, carry) + cb_ref[c]
        for r8 in range(0, rt, SUBLANES):
            y_ref[pl.ds(r8 * CONV_ROW_PITCH + c, SUBLANES, stride=CONV_ROW_PITCH), :] = y[r8:r8 + SUBLANES]

    @pl.when(g == pl.num_programs(1) - 1)
    def _():
        for r in range(rt):
            x = y_ref[r * CONV_ROW_PITCH:r * CONV_ROW_PITCH + CONV_CH, :]
            o_ref[r * CONV_BLK:(r + 1) * CONV_BLK, :] = x.T.astype(o_ref.dtype)


def _conv_module(ut, conv_w, conv_b):
    b, c, s = ut.shape
    rows_per_seq = s // CONV_BLK
    n_rows = b * rows_per_seq
    row_tile = math.gcd(n_rows, CONV_ROW_TILE)
    assert row_tile % rows_per_seq == 0
    a = ut.reshape(b, c, rows_per_seq, CONV_BLK).transpose(1, 0, 2, 3).reshape(c, n_rows, CONV_BLK)
    taps = jnp.pad(conv_w[::-1].T.astype(F32), ((0, 0), (0, 2 * CONV_BLK - CONV_WIDTH)))
    return pl.pallas_call(
        functools.partial(_conv_kernel, rows_per_seq),
        grid=(n_rows // row_tile, c // CONV_CH_TILE),
        in_specs=[pl.BlockSpec(memory_space=pltpu.SMEM),
                  pl.BlockSpec((CONV_CH_TILE, row_tile, CONV_BLK), lambda i, g: (g, i, 0)),
                  pl.BlockSpec((CONV_CH_TILE, 2 * CONV_BLK), lambda i, g: (g, 0))],
        out_specs=pl.BlockSpec((row_tile * CONV_BLK, c), lambda i, g: (i, 0)),
        out_shape=jax.ShapeDtypeStruct((b * s, c), BF16),
        scratch_shapes=[pltpu.VMEM((row_tile * CONV_ROW_PITCH, CONV_BLK), F32)],
        compiler_params=pltpu.CompilerParams(
            dimension_semantics=("arbitrary", "arbitrary"), vmem_limit_bytes=VMEM_LIMIT),
        name="conv_module",
    )(conv_b.astype(F32), a, taps)


def _mix_ffn_kernel(alpha, x_ref, att_ref, cv_ref, gate_ref, p_ref,
                    cg_ref, cb_ref, wa_ref, wc_ref, wm_ref, g1_ref, b1_ref,
                    wg_ref, wu_ref, wd_ref, wpg_ref, bpg_ref, wp_ref, g2_ref, b2_ref, o_ref):
    tm, d = x_ref.shape
    unit = tm // sum(MIX_ROW_FRACTIONS)
    bounds = np.cumsum((0,) + MIX_ROW_FRACTIONS) * unit
    groups = [slice(int(lo), int(hi)) for lo, hi in zip(bounds[:-1], bounds[1:])]
    dot = functools.partial(jnp.dot, preferred_element_type=F32)
    y_att = [dot(att_ref[rs, :], wa_ref[...]) for rs in groups]
    cvn = [_layer_norm(cv_ref[rs, :].astype(F32), cg_ref[...], cb_ref[...]) for rs in groups]
    cvn = [(v * jax.nn.sigmoid(v)).astype(BF16) for v in cvn]
    y_conv = [dot(v, wc_ref[...]) for v in cvn]
    merged = [(gate_ref[rs, :d].astype(F32) * ya + gate_ref[rs, d:].astype(F32) * yc).astype(BF16)
              for rs, ya, yc in zip(groups, y_att, y_conv)]
    mixed = [dot(m, wm_ref[...]) for m in merged]
    pe = [dot(p_ref[rs, :].astype(BF16), wp_ref[...]) for rs in groups]
    x1 = [_layer_norm(alpha * x_ref[rs, :] + mx, g1_ref[...], b1_ref[...]) for rs, mx in zip(groups, mixed)]
    xb = [v.astype(BF16) for v in x1]
    hg = [dot(v, wg_ref[...]) for v in xb]
    hu = [dot(v, wu_ref[...]) for v in xb]
    pg = [dot(v, wpg_ref[...]) for v in xb]
    hid = [(g * jax.nn.sigmoid(g) * u).astype(BF16) for g, u in zip(hg, hu)]
    ffn = [dot(v, wd_ref[...]) for v in hid]
    for h, rs in enumerate(groups):
        ple = jax.nn.sigmoid(pg[h] + bpg_ref[...]) * pe[h]
        o_ref[rs, :] = _layer_norm(alpha * x1[h] + ffn[h] + ple, g2_ref[...], b2_ref[...])


def _mix_ffn(alpha, x, att, cv, gates, p, weights, tm):
    t, d = x.shape
    row = lambda n: pl.BlockSpec((tm, n), lambda i: (i, 0))
    return pl.pallas_call(
        functools.partial(_mix_ffn_kernel, alpha),
        grid=(t // tm,),
        in_specs=[row(d), row(ATT_W), row(CONV_CH), row(2 * d), row(PLE_DIM)]
                 + [_resident(w.shape) for w in weights],
        out_specs=row(d),
        out_shape=jax.ShapeDtypeStruct((t, d), F32),
        compiler_params=pltpu.CompilerParams(
            dimension_semantics=("arbitrary",), vmem_limit_bytes=VMEM_LIMIT),
        name="mix_ffn",
    )(x, att, cv, gates, p, *weights)


def kernel(x, p, w_in, b_gate, bias_table, w_att_out, conv_w, conv_b, conv_ln_g, conv_ln_b,
           w_conv_out, w_mix_out, ln_mix_g, ln_mix_b, w_ffn_gate, w_ffn_up, w_ffn_down,
           w_ple, w_ple_gate, b_ple_gate, ln_ffn_g, ln_ffn_b):
    b, s, d = x.shape
    depth = w_in.shape[0]
    assert d == D_MODEL and s % MOBA_BLOCK == 0
    alpha = (2.0 * depth) ** 0.25
    tm = 1024
    tm_ffn = 512
    row = lambda v: v.reshape(1, -1).astype(F32)
    k_end, v_end = 2 * ATT_W, 3 * ATT_W
    u_end = v_end + 2 * CONV_CH

    tbl_t = bias_table.T.astype(F32)
    bias_own, bias_prev = _bias_tables(tbl_t)
    for i in range(depth):
        w = w_in[i].astype(BF16)
        qk, vt, ut, gates = _proj(x, w[:, :k_end], w[:, k_end:v_end].T, w[:, v_end:u_end].T,
                                  w[:, u_end:], row(b_gate[i]), tm)
        att = _attention(tbl_t, qk, vt, bias_own, bias_prev)
        cv = _conv_module(ut, conv_w[i], conv_b[i])
        weights = (row(conv_ln_g[i]), row(conv_ln_b[i]), w_att_out[i].astype(BF16), w_conv_out[i].astype(BF16), w_mix_out[i].astype(BF16),
                   row(ln_mix_g[i]), row(ln_mix_b[i]),
                   w_ffn_gate[i].astype(BF16), w_ffn_up[i].astype(BF16), w_ffn_down[i].astype(BF16),
                   w_ple_gate[i].astype(BF16), row(b_ple_gate[i]), w_ple[i].astype(BF16),
                   row(ln_ffn_g[i]), row(ln_ffn_b[i]))
        x2 = _mix_ffn(alpha, x.reshape(b * s, d), att.reshape(b * s, ATT_W), cv,
                      gates.reshape(b * s, 2 * d), p[i].reshape(b * s, PLE_DIM), weights, tm_ffn)
        x = x2.reshape(b, s, d)
    return x
```

```python
import functools
import math

import numpy as np
import jax
import jax.numpy as jnp
from jax import lax
from jax.experimental import pallas as pl
from jax.experimental.pallas import tpu as pltpu

D_MODEL = 1024
PLE_DIM = 256
ATT_HEADS = 8
HEAD_DIM = 64
ATT_W = ATT_HEADS * HEAD_DIM
MOBA_BLOCK = 256
MOBA_TOPK = 3
CONV_CH = 512
CONV_WIDTH = 31
NUM_BUCKETS = 32
MAX_DISTANCE = 128
FFN_HIDDEN = 2816
LN_EPS = 1e-5
NEG_INF = -1e30

LANES = 128
SUBLANES = 8
HEADS_PER_STEP = LANES // HEAD_DIM
ONES_ROWS = 16
LOG2E = math.log2(math.e)
ATTN_SLOTS = 4
ATTN_ROWS = 2
PROJ_ROW_SPLIT = 1
MIX_ROW_FRACTIONS = (1, 1)
CONV_BLK = LANES
CONV_ROW_TILE = 64
CONV_CH_TILE = 64
CONV_ROW_PITCH = CONV_CH + 8
VMEM_LIMIT = 56 * 1024 * 1024

BF16 = jnp.bfloat16
F32 = jnp.float32
NT_DIMS = (((1,), (1,)), ((), ()))


def _resident(shape):
    return pl.BlockSpec(shape, lambda *_: (0,) * len(shape), pipeline_mode=pl.Buffered(1))


def _layer_norm(h, g, b):
    mu = jnp.mean(h, axis=-1, keepdims=True)
    d = h - mu
    var = jnp.mean(d * d, axis=-1, keepdims=True)
    return d * lax.rsqrt(var + LN_EPS) * g + b


def _t5_bucket_np(rel):
    n = np.maximum(rel, 0)
    max_exact = NUM_BUCKETS // 2
    nf = np.maximum(n, 1).astype(np.float32)
    large = max_exact + (np.log(nf / np.float32(max_exact)) / np.float32(math.log(MAX_DISTANCE / max_exact))
                         * np.float32(NUM_BUCKETS - max_exact)).astype(np.int32)
    large = np.minimum(large, NUM_BUCKETS - 1)
    return np.where(n < max_exact, n, large).astype(np.int32)


def _bias_bucket_tables():
    kpos = np.arange(MOBA_BLOCK)[:, None]
    qpos = np.arange(MOBA_BLOCK)[None, :]
    rel_own = qpos - kpos
    own = np.where(rel_own >= 0, _t5_bucket_np(rel_own), NUM_BUCKETS)
    prev = _t5_bucket_np(rel_own + MOBA_BLOCK)
    return own.astype(np.int32), prev.astype(np.int32)


def _bias_kernel(tbl_ref, own_ix_ref, prev_ix_ref, own_ref, prev_ref):
    h = pl.program_id(0)
    own_ix = own_ix_ref[...]
    prev_ix = prev_ix_ref[...]
    own = jnp.full(own_ix.shape, NEG_INF, F32)
    prev = jnp.zeros(prev_ix.shape, F32)
    for b in range(NUM_BUCKETS):
        t = tbl_ref[h, b] * LOG2E
        own = jnp.where(own_ix == b, t, own)
        prev = jnp.where(prev_ix == b, t, prev)
    own_ref[0] = own
    prev_ref[0] = prev


def _bias_tables(tbl_t):
    own_ix, prev_ix = _bias_bucket_tables()
    blk = (MOBA_BLOCK, MOBA_BLOCK)
    return pl.pallas_call(
        _bias_kernel,
        grid=(ATT_HEADS,),
        in_specs=[pl.BlockSpec(memory_space=pltpu.SMEM),
                  pl.BlockSpec(blk, lambda h: (0, 0)),
                  pl.BlockSpec(blk, lambda h: (0, 0))],
        out_specs=[pl.BlockSpec((1,) + blk, lambda h: (h, 0, 0)),
                   pl.BlockSpec((1,) + blk, lambda h: (h, 0, 0))],
        out_shape=[jax.ShapeDtypeStruct((ATT_HEADS,) + blk, F32)] * 2,
        name="t5_bias",
    )(tbl_t, jnp.asarray(own_ix), jnp.asarray(prev_ix))


def _proj_kernel(x_ref, wqk_ref, wvt_ref, wagt_ref, wgate_ref, bgate_ref,
                 qk_ref, vt_ref, ut_ref, gate_ref):
    tm = x_ref.shape[1]
    groups = [slice(h * (tm // PROJ_ROW_SPLIT), (h + 1) * (tm // PROJ_ROW_SPLIT)) for h in range(PROJ_ROW_SPLIT)]
    xb = [x_ref[0, rs, :].astype(BF16) for rs in groups]
    for rs, v in zip(groups, xb):
        gl = jnp.dot(v, wgate_ref[...], preferred_element_type=F32) + bgate_ref[...]
        gate_ref[0, rs, :] = jax.nn.sigmoid(gl).astype(BF16)
    for rs, v in zip(groups, xb):
        agt = lax.dot_general(wagt_ref[...], v, NT_DIMS, preferred_element_type=F32)
        ut_ref[0, :, rs] = (agt[:CONV_CH] * jax.nn.sigmoid(agt[CONV_CH:])).astype(BF16)
    for rs, v in zip(groups, xb):
        qk = jnp.dot(v, wqk_ref[...], preferred_element_type=F32)
        qk_ref[0, rs, :ATT_W] = (qk[:, :ATT_W] * (LOG2E * HEAD_DIM ** -0.5)).astype(BF16)
        qk_ref[0, rs, ATT_W:] = qk[:, ATT_W:].astype(BF16)
    for rs, v in zip(groups, xb):
        vt = lax.dot_general(wvt_ref[...], v, NT_DIMS, preferred_element_type=F32)
        vt_ref[0, :, rs] = vt.astype(BF16)


def _proj(x, wqk, wvt, wagt, wgate, bgate, tm):
    b, s, d = x.shape
    return pl.pallas_call(
        _proj_kernel,
        grid=(b, s // tm),
        in_specs=[pl.BlockSpec((1, tm, d), lambda i, j: (i, j, 0)),
                  _resident(wqk.shape), _resident(wvt.shape), _resident(wagt.shape),
                  _resident(wgate.shape), _resident(bgate.shape)],
        out_specs=[pl.BlockSpec((1, tm, 2 * ATT_W), lambda i, j: (i, j, 0)),
                   pl.BlockSpec((1, ATT_W, tm), lambda i, j: (i, 0, j)),
                   pl.BlockSpec((1, CONV_CH, tm), lambda i, j: (i, 0, j)),
                   pl.BlockSpec((1, tm, 2 * d), lambda i, j: (i, j, 0))],
        out_shape=[jax.ShapeDtypeStruct((b, s, 2 * ATT_W), BF16),
                   jax.ShapeDtypeStruct((b, ATT_W, s), BF16),
                   jax.ShapeDtypeStruct((b, CONV_CH, s), BF16),
                   jax.ShapeDtypeStruct((b, s, 2 * d), BF16)],
        compiler_params=pltpu.CompilerParams(
            dimension_semantics=("arbitrary", "arbitrary"), vmem_limit_bytes=VMEM_LIMIT),
        name="in_proj",
    )(x, wqk, wvt, wagt, wgate, bgate)


def _attn_kernel(tbl_ref, q_ref, k_ref, vt_ref, bown_ref, bprev_ref, o_ref,
                 qm_ref, vaug_ref, *sp_refs):
    s_refs, p_refs = sp_refs[:ATTN_SLOTS], sp_refs[ATTN_SLOTS:]
    hp = pl.program_id(0)
    n_rows, s, _ = q_ref.shape
    nb = s // MOBA_BLOCK
    lane = lax.broadcasted_iota(jnp.int32, (1, LANES), 1)
    q_blk = lax.broadcasted_iota(jnp.int32, (1, s), 1) // MOBA_BLOCK
    blk = lax.broadcasted_iota(jnp.int32, (nb, 1), 0)
    past = blk < q_blk
    far_bias = [tbl_ref[hp * HEADS_PER_STEP + hh, NUM_BUCKETS - 1] * LOG2E for hh in range(HEADS_PER_STEP)]

    sel_neg = {}
    for r in range(n_rows):
        q = q_ref[r]
        k = k_ref[r]
        k_mean = (jnp.sum(k.astype(F32).reshape(nb, MOBA_BLOCK, LANES), axis=1)
                  * (1.0 / MOBA_BLOCK)).astype(BF16)
        for hh in range(HEADS_PER_STEP):
            rh = r * HEADS_PER_STEP + hh
            qm = jnp.where(lane // HEAD_DIM == hh, q, jnp.zeros_like(q))
            qm_ref[rh] = qm
            vaug_ref[rh, :HEAD_DIM, :] = vt_ref[r, hh * HEAD_DIM:(hh + 1) * HEAD_DIM, :]
            vaug_ref[rh, HEAD_DIM:, :] = jnp.ones((ONES_ROWS, s), BF16)
            gate = lax.dot_general(k_mean, qm, NT_DIMS, preferred_element_type=F32)
            rank = jnp.zeros((nb, s), jnp.int32)
            for m in range(nb):
                gm = gate[m:m + 1, :]
                beats = (gm > gate) | ((gm == gate) & (m < blk))
                rank = rank + jnp.where(beats & (m < q_blk), 1, 0)
            sel_neg[r, hh] = jnp.where(past & (rank < MOBA_TOPK), 0.0, NEG_INF).astype(F32)

    rows = lambda j: slice(j * MOBA_BLOCK, (j + 1) * MOBA_BLOCK)
    chains = [(i, r, hh) for i in range(nb) for r in range(n_rows) for hh in range(HEADS_PER_STEP)]
    slot_of = {chain: c % ATTN_SLOTS for c, chain in enumerate(chains)}
    shifts = {}
    outs = {}

    def pass_a(i, r, hh):
        hi = (i + 1) * MOBA_BLOCK
        slot = slot_of[i, r, hh]
        st = lax.dot_general(k_ref[r, :hi, :], qm_ref[r * HEADS_PER_STEP + hh, rows(i), :], NT_DIMS,
                             preferred_element_type=F32)
        col_max, offs = None, []
        for j in range(i + 1):
            blk_t = st[rows(j)]
            if j == i:
                blk_t = blk_t + bown_ref[hh]
                off = None
            elif j == i - 1:
                blk_t = blk_t + bprev_ref[hh]
                off = sel_neg[r, hh][j:j + 1, rows(i)]
            else:
                off = sel_neg[r, hh][j:j + 1, rows(i)] + far_bias[hh]
            s_refs[slot][rows(j), :] = blk_t
            cand = jnp.max(blk_t, axis=0, keepdims=True)
            cand = cand if off is None else cand + off
            col_max = cand if col_max is None else jnp.maximum(col_max, cand)
            offs.append(off)
        shifts[i, r, hh] = [col_max if off is None else col_max - off for off in offs]

    def pass_b(i, r, hh):
        hi = (i + 1) * MOBA_BLOCK
        slot = slot_of[i, r, hh]
        shift = shifts.pop((i, r, hh))
        for j in range(i + 1):
            p_refs[slot][rows(j), :] = jnp.exp2(s_refs[slot][rows(j), :] - shift[j]).astype(BF16)
        o_aug = jnp.dot(vaug_ref[r * HEADS_PER_STEP + hh, :, :hi], p_refs[slot][:hi, :],
                        preferred_element_type=F32)
        outs[hh] = o_aug[:HEAD_DIM] / o_aug[HEAD_DIM:HEAD_DIM + 1]
        if hh == HEADS_PER_STEP - 1:
            o_pair = [outs.pop(h2) for h2 in range(HEADS_PER_STEP)]
            o_ref[r, rows(i), :] = jnp.concatenate(o_pair, axis=0).T.astype(o_ref.dtype)

    for c in range(len(chains) + ATTN_SLOTS - 1):
        if c < len(chains):
            pass_a(*chains[c])
        if c >= ATTN_SLOTS - 1:
            pass_b(*chains[c - (ATTN_SLOTS - 1)])


def _attention(tbl_t, qk, vt, bias_own, bias_prev):
    b, s, _ = qk.shape
    n_groups = ATT_W // LANES
    n_rows = math.gcd(b, ATTN_ROWS)
    blk = (HEADS_PER_STEP, MOBA_BLOCK, MOBA_BLOCK)
    return pl.pallas_call(
        _attn_kernel,
        grid=(n_groups, b // n_rows),
        in_specs=[pl.BlockSpec(memory_space=pltpu.SMEM),
                  pl.BlockSpec((n_rows, s, LANES), lambda g, i: (i, 0, g)),
                  pl.BlockSpec((n_rows, s, LANES), lambda g, i: (i, 0, n_groups + g)),
                  pl.BlockSpec((n_rows, LANES, s), lambda g, i: (i, g, 0)),
                  pl.BlockSpec(blk, lambda g, i: (g, 0, 0)),
                  pl.BlockSpec(blk, lambda g, i: (g, 0, 0))],
        out_specs=pl.BlockSpec((n_rows, s, LANES), lambda g, i: (i, 0, g)),
        out_shape=jax.ShapeDtypeStruct((b, s, ATT_W), BF16),
        scratch_shapes=[pltpu.VMEM((n_rows * HEADS_PER_STEP, s, LANES), BF16),
                        pltpu.VMEM((n_rows * HEADS_PER_STEP, HEAD_DIM + ONES_ROWS, s), BF16),
                        *[pltpu.VMEM((s, MOBA_BLOCK), F32)] * ATTN_SLOTS,
                        *[pltpu.VMEM((s, MOBA_BLOCK), BF16)] * ATTN_SLOTS],
        compiler_params=pltpu.CompilerParams(
            dimension_semantics=("arbitrary", "arbitrary"), vmem_limit_bytes=VMEM_LIMIT),
        name="moba_attn",
    )(tbl_t, qk, qk, vt, bias_own, bias_prev)


def _conv_kernel(rows_per_seq, cb_ref, a_ref, taps_ref, o_ref, y_ref):
    g = pl.program_id(1)
    ct, rt, _ = a_ref.shape
    first_blk = lax.broadcasted_iota(jnp.int32, (rt, 1), 0) % rows_per_seq == 0
    for cl in range(ct):
        c = g * ct + cl
        taps = jnp.broadcast_to(taps_ref[cl:cl + 1, :], (CONV_BLK, 2 * CONV_BLK))
        w_c = pltpu.roll(taps, 0, 1, stride=1, stride_axis=0).astype(BF16)
        res = jnp.dot(a_ref[cl], w_c, preferred_element_type=F32)
        carry = pltpu.roll(res[:, CONV_BLK:], 1, 0)
        y = res[:, :CONV_BLK] + jnp.where(first_blk, 0.0, carry) + cb_ref[c]
        for r8 in range(0, rt, SUBLANES):
            y_ref[pl.ds(r8 * CONV_ROW_PITCH + c, SUBLANES, stride=CONV_ROW_PITCH), :] = y[r8:r8 + SUBLANES]

    @pl.when(g == pl.num_programs(1) - 1)
    def _():
        for r in range(rt):
            x = y_ref[r * CONV_ROW_PITCH:r * CONV_ROW_PITCH + CONV_CH, :]
            o_ref[r * CONV_BLK:(r + 1) * CONV_BLK, :] = x.T.astype(o_ref.dtype)


def _conv_module(ut, conv_w, conv_b):
    b, c, s = ut.shape
    rows_per_seq = s // CONV_BLK
    n_rows = b * rows_per_seq
    row_tile = math.gcd(n_rows, CONV_ROW_TILE)
    assert row_tile % rows_per_seq == 0
    a = ut.reshape(b, c, rows_per_seq, CONV_BLK).transpose(1, 0, 2, 3).reshape(c, n_rows, CONV_BLK)
    taps = jnp.pad(conv_w[::-1].T.astype(F32), ((0, 0), (0, 2 * CONV_BLK - CONV_WIDTH)))
    return pl.pallas_call(
        functools.partial(_conv_kernel, rows_per_seq),
        grid=(n_rows // row_tile, c // CONV_CH_TILE),
        in_specs=[pl.BlockSpec(memory_space=pltpu.SMEM),
                  pl.BlockSpec((CONV_CH_TILE, row_tile, CONV_BLK), lambda i, g: (g, i, 0)),
                  pl.BlockSpec((CONV_CH_TILE, 2 * CONV_BLK), lambda i, g: (g, 0))],
        out_specs=pl.BlockSpec((row_tile * CONV_BLK, c), lambda i, g: (i, 0)),
        out_shape=jax.ShapeDtypeStruct((b * s, c), BF16),
        scratch_shapes=[pltpu.VMEM((row_tile * CONV_ROW_PITCH, CONV_BLK), F32)],
        compiler_params=pltpu.CompilerParams(
            dimension_semantics=("arbitrary", "arbitrary"), vmem_limit_bytes=VMEM_LIMIT),
        name="conv_module",
    )(conv_b.astype(F32), a, taps)


def _mix_ffn_kernel(alpha, x_ref, att_ref, cv_ref, gate_ref, p_ref,
                    cg_ref, cb_ref, wa_ref, wc_ref, wm_ref, g1_ref, b1_ref,
                    wg_ref, wu_ref, wd_ref, wpg_ref, bpg_ref, wp_ref, g2_ref, b2_ref, o_ref):
    tm, d = x_ref.shape
    unit = tm // sum(MIX_ROW_FRACTIONS)
    bounds = np.cumsum((0,) + MIX_ROW_FRACTIONS) * unit
    groups = [slice(int(lo), int(hi)) for lo, hi in zip(bounds[:-1], bounds[1:])]
    dot = functools.partial(jnp.dot, preferred_element_type=F32)
    y_att = [dot(att_ref[rs, :], wa_ref[...]) for rs in groups]
    cvn = [_layer_norm(cv_ref[rs, :].astype(F32), cg_ref[...], cb_ref[...]) for rs in groups]
    cvn = [(v * jax.nn.sigmoid(v)).astype(BF16) for v in cvn]
    y_conv = [dot(v, wc_ref[...]) for v in cvn]
    merged = [(gate_ref[rs, :d].astype(F32) * ya + gate_ref[rs, d:].astype(F32) * yc).astype(BF16)
              for rs, ya, yc in zip(groups, y_att, y_conv)]
    mixed = [dot(m, wm_ref[...]) for m in merged]
    pe = [dot(p_ref[rs, :].astype(BF16), wp_ref[...]) for rs in groups]
    x1 = [_layer_norm(alpha * x_ref[rs, :] + mx, g1_ref[...], b1_ref[...]) for rs, mx in zip(groups, mixed)]
    xb = [v.astype(BF16) for v in x1]
    hg = [dot(v, wg_ref[...]) for v in xb]
    hu = [dot(v, wu_ref[...]) for v in xb]
    pg = [dot(v, wpg_ref[...]) for v in xb]
    hid = [(g * jax.nn.sigmoid(g) * u).astype(BF16) for g, u in zip(hg, hu)]
    ffn = [dot(v, wd_ref[...]) for v in hid]
    for h, rs in enumerate(groups):
        ple = jax.nn.sigmoid(pg[h] + bpg_ref[...]) * pe[h]
        o_ref[rs, :] = _layer_norm(alpha * x1[h] + ffn[h] + ple, g2_ref[...], b2_ref[...])


def _mix_ffn(alpha, x, att, cv, gates, p, weights, tm):
    t, d = x.shape
    row = lambda n: pl.BlockSpec((tm, n), lambda i: (i, 0))
    return pl.pallas_call(
        functools.partial(_mix_ffn_kernel, alpha),
        grid=(t // tm,),
        in_specs=[row(d), row(ATT_W), row(CONV_CH), row(2 * d), row(PLE_DIM)]
                 + [_resident(w.shape) for w in weights],
        out_specs=row(d),
        out_shape=jax.ShapeDtypeStruct((t, d), F32),
        compiler_params=pltpu.CompilerParams(
            dimension_semantics=("arbitrary",), vmem_limit_bytes=VMEM_LIMIT),
        name="mix_ffn",
    )(x, att, cv, gates, p, *weights)


def kernel(x, p, w_in, b_gate, bias_table, w_att_out, conv_w, conv_b, conv_ln_g, conv_ln_b,
           w_conv_out, w_mix_out, ln_mix_g, ln_mix_b, w_ffn_gate, w_ffn_up, w_ffn_down,
           w_ple, w_ple_gate, b_ple_gate, ln_ffn_g, ln_ffn_b):
    b, s, d = x.shape
    depth = w_in.shape[0]
    assert d == D_MODEL and s % MOBA_BLOCK == 0
    alpha = (2.0 * depth) ** 0.25
    tm = 1024
    tm_ffn = 512
    row = lambda v: v.reshape(1, -1).astype(F32)
    k_end, v_end = 2 * ATT_W, 3 * ATT_W
    u_end = v_end + 2 * CONV_CH

    tbl_t = bias_table.T.astype(F32)
    bias_own, bias_prev = _bias_tables(tbl_t)
    for i in range(depth):
        w = w_in[i].astype(BF16)
        qk, vt, ut, gates = _proj(x, w[:, :k_end], w[:, k_end:v_end].T, w[:, v_end:u_end].T,
                                  w[:, u_end:], row(b_gate[i]), tm)
        att = _attention(tbl_t, qk, vt, bias_own, bias_prev)
        cv = _conv_module(ut, conv_w[i], conv_b[i])
        weights = (row(conv_ln_g[i]), row(conv_ln_b[i]), w_att_out[i].astype(BF16), w_conv_out[i].astype(BF16), w_mix_out[i].astype(BF16),
                   row(ln_mix_g[i]), row(ln_mix_b[i]),
                   w_ffn_gate[i].astype(BF16), w_ffn_up[i].astype(BF16), w_ffn_down[i].astype(BF16),
                   w_ple_gate[i].astype(BF16), row(b_ple_gate[i]), w_ple[i].astype(BF16),
                   row(ln_ffn_g[i]), row(ln_ffn_b[i]))
        x2 = _mix_ffn(alpha, x.reshape(b * s, d), att.reshape(b * s, ATT_W), cv,
                      gates.reshape(b * s, 2 * d), p[i].reshape(b * s, PLE_DIM), weights, tm_ffn)
        x = x2.reshape(b, s, d)
    return x
```

```python
import functools
import math

import numpy as np
import jax
import jax.numpy as jnp
from jax import lax
from jax.experimental import pallas as pl
from jax.experimental.pallas import tpu as pltpu

D_MODEL = 1024
PLE_DIM = 256
ATT_HEADS = 8
HEAD_DIM = 64
ATT_W = ATT_HEADS * HEAD_DIM
MOBA_BLOCK = 256
MOBA_TOPK = 3
CONV_CH = 512
CONV_WIDTH = 31
NUM_BUCKETS = 32
MAX_DISTANCE = 128
FFN_HIDDEN = 2816
LN_EPS = 1e-5
NEG_INF = -1e30

LANES = 128
SUBLANES = 8
HEADS_PER_STEP = LANES // HEAD_DIM
ONES_ROWS = 16
LOG2E = math.log2(math.e)
ATTN_SLOTS = 4
ATTN_ROWS = 2
PROJ_ROW_SPLIT = 1
MIX_ROW_FRACTIONS = (1, 1)
CONV_BLK = LANES
CONV_ROW_TILE = 64
CONV_CH_TILE = 64
CONV_ROW_PITCH = CONV_CH + 8
VMEM_LIMIT = 56 * 1024 * 1024

BF16 = jnp.bfloat16
F32 = jnp.float32
NT_DIMS = (((1,), (1,)), ((), ()))


def _resident(shape):
    return pl.BlockSpec(shape, lambda *_: (0,) * len(shape), pipeline_mode=pl.Buffered(1))


def _layer_norm(h, g, b):
    mu = jnp.mean(h, axis=-1, keepdims=True)
    d = h - mu
    var = jnp.mean(d * d, axis=-1, keepdims=True)
    return d * lax.rsqrt(var + LN_EPS) * g + b


def _t5_bucket_np(rel):
    n = np.maximum(rel, 0)
    max_exact = NUM_BUCKETS // 2
    nf = np.maximum(n, 1).astype(np.float32)
    large = max_exact + (np.log(nf / np.float32(max_exact)) / np.float32(math.log(MAX_DISTANCE / max_exact))
                         * np.float32(NUM_BUCKETS - max_exact)).astype(np.int32)
    large = np.minimum(large, NUM_BUCKETS - 1)
    return np.where(n < max_exact, n, large).astype(np.int32)


def _bias_bucket_tables():
    kpos = np.arange(MOBA_BLOCK)[:, None]
    qpos = np.arange(MOBA_BLOCK)[None, :]
    rel_own = qpos - kpos
    own = np.where(rel_own >= 0, _t5_bucket_np(rel_own), NUM_BUCKETS)
    prev = _t5_bucket_np(rel_own + MOBA_BLOCK)
    return own.astype(np.int32), prev.astype(np.int32)


def _bias_kernel(tbl_ref, own_ix_ref, prev_ix_ref, own_ref, prev_ref):
    h = pl.program_id(0)
    own_ix = own_ix_ref[...]
    prev_ix = prev_ix_ref[...]
    own = jnp.full(own_ix.shape, NEG_INF, F32)
    prev = jnp.zeros(prev_ix.shape, F32)
    for b in range(NUM_BUCKETS):
        t = tbl_ref[h, b] * LOG2E
        own = jnp.where(own_ix == b, t, own)
        prev = jnp.where(prev_ix == b, t, prev)
    own_ref[0] = own
    prev_ref[0] = prev


def _bias_tables(tbl_t):
    own_ix, prev_ix = _bias_bucket_tables()
    blk = (MOBA_BLOCK, MOBA_BLOCK)
    return pl.pallas_call(
        _bias_kernel,
        grid=(ATT_HEADS,),
        in_specs=[pl.BlockSpec(memory_space=pltpu.SMEM),
                  pl.BlockSpec(blk, lambda h: (0, 0)),
                  pl.BlockSpec(blk, lambda h: (0, 0))],
        out_specs=[pl.BlockSpec((1,) + blk, lambda h: (h, 0, 0)),
                   pl.BlockSpec((1,) + blk, lambda h: (h, 0, 0))],
        out_shape=[jax.ShapeDtypeStruct((ATT_HEADS,) + blk, F32)] * 2,
        name="t5_bias",
    )(tbl_t, jnp.asarray(own_ix), jnp.asarray(prev_ix))


def _proj_kernel(x_ref, wqk_ref, wvt_ref, wagt_ref, wgate_ref, bgate_ref,
                 qk_ref, vt_ref, ut_ref, gate_ref):
    tm = x_ref.shape[1]
    groups = [slice(h * (tm // PROJ_ROW_SPLIT), (h + 1) * (tm // PROJ_ROW_SPLIT)) for h in range(PROJ_ROW_SPLIT)]
    xb = [x_ref[0, rs, :].astype(BF16) for rs in groups]
    for rs, v in zip(groups, xb):
        gl = jnp.dot(v, wgate_ref[...], preferred_element_type=F32) + bgate_ref[...]
        gate_ref[0, rs, :] = jax.nn.sigmoid(gl).astype(BF16)
    for rs, v in zip(groups, xb):
        agt = lax.dot_general(wagt_ref[...], v, NT_DIMS, preferred_element_type=F32)
        ut_ref[0, :, rs] = (agt[:CONV_CH] * jax.nn.sigmoid(agt[CONV_CH:])).astype(BF16)
    for rs, v in zip(groups, xb):
        qk = jnp.dot(v, wqk_ref[...], preferred_element_type=F32)
        qk_ref[0, rs, :ATT_W] = (qk[:, :ATT_W] * (LOG2E * HEAD_DIM ** -0.5)).astype(BF16)
        qk_ref[0, rs, ATT_W:] = qk[:, ATT_W:].astype(BF16)
    for rs, v in zip(groups, xb):
        vt = lax.dot_general(wvt_ref[...], v, NT_DIMS, preferred_element_type=F32)
        vt_ref[0, :, rs] = vt.astype(BF16)


def _proj(x, wqk, wvt, wagt, wgate, bgate, tm):
    b, s, d = x.shape
    return pl.pallas_call(
        _proj_kernel,
        grid=(b, s // tm),
        in_specs=[pl.BlockSpec((1, tm, d), lambda i, j: (i, j, 0)),
                  _resident(wqk.shape), _resident(wvt.shape), _resident(wagt.shape),
                  _resident(wgate.shape), _resident(bgate.shape)],
        out_specs=[pl.BlockSpec((1, tm, 2 * ATT_W), lambda i, j: (i, j, 0)),
                   pl.BlockSpec((1, ATT_W, tm), lambda i, j: (i, 0, j)),
                   pl.BlockSpec((1, CONV_CH, tm), lambda i, j: (i, 0, j)),
                   pl.BlockSpec((1, tm, 2 * d), lambda i, j: (i, j, 0))],
        out_shape=[jax.ShapeDtypeStruct((b, s, 2 * ATT_W), BF16),
                   jax.ShapeDtypeStruct((b, ATT_W, s), BF16),
                   jax.ShapeDtypeStruct((b, CONV_CH, s), BF16),
                   jax.ShapeDtypeStruct((b, s, 2 * d), BF16)],
        compiler_params=pltpu.CompilerParams(
            dimension_semantics=("arbitrary", "arbitrary"), vmem_limit_bytes=VMEM_LIMIT),
        name="in_proj",
    )(x, wqk, wvt, wagt, wgate, bgate)


def _attn_kernel(tbl_ref, q_ref, k_ref, vt_ref, bown_ref, bprev_ref, o_ref,
                 qm_ref, vaug_ref, *sp_refs):
    s_refs, p_refs = sp_refs[:ATTN_SLOTS], sp_refs[ATTN_SLOTS:]
    hp = pl.program_id(0)
    n_rows, s, _ = q_ref.shape
    nb = s // MOBA_BLOCK
    lane = lax.broadcasted_iota(jnp.int32, (1, LANES), 1)
    q_blk = lax.broadcasted_iota(jnp.int32, (1, s), 1) // MOBA_BLOCK
    blk = lax.broadcasted_iota(jnp.int32, (nb, 1), 0)
    past = blk < q_blk
    far_bias = [tbl_ref[hp * HEADS_PER_STEP + hh, NUM_BUCKETS - 1] * LOG2E for hh in range(HEADS_PER_STEP)]

    sel_neg = {}
    for r in range(n_rows):
        q = q_ref[r]
        k = k_ref[r]
        k_mean = (jnp.sum(k.astype(F32).reshape(nb, MOBA_BLOCK, LANES), axis=1)
                  * (1.0 / MOBA_BLOCK)).astype(BF16)
        for hh in range(HEADS_PER_STEP):
            rh = r * HEADS_PER_STEP + hh
            qm = jnp.where(lane // HEAD_DIM == hh, q, jnp.zeros_like(q))
            qm_ref[rh] = qm
            vaug_ref[rh, :HEAD_DIM, :] = vt_ref[r, hh * HEAD_DIM:(hh + 1) * HEAD_DIM, :]
            vaug_ref[rh, HEAD_DIM:, :] = jnp.ones((ONES_ROWS, s), BF16)
            gate = lax.dot_general(k_mean, qm, NT_DIMS, preferred_element_type=F32)
            rank = jnp.zeros((nb, s), jnp.int32)
            for m in range(nb):
                gm = gate[m:m + 1, :]
                beats = (gm > gate) | ((gm == gate) & (m < blk))
                rank = rank + jnp.where(beats & (m < q_blk), 1, 0)
            sel_neg[r, hh] = jnp.where(past & (rank < MOBA_TOPK), 0.0, NEG_INF).astype(F32)

    rows = lambda j: slice(j * MOBA_BLOCK, (j + 1) * MOBA_BLOCK)
    chains = [(i, r, hh) for i in range(nb) for r in range(n_rows) for hh in range(HEADS_PER_STEP)]
    slot_of = {chain: c % ATTN_SLOTS for c, chain in enumerate(chains)}
    shifts = {}
    outs = {}

    def pass_a(i, r, hh):
        hi = (i + 1) * MOBA_BLOCK
        slot = slot_of[i, r, hh]
        st = lax.dot_general(k_ref[r, :hi, :], qm_ref[r * HEADS_PER_STEP + hh, rows(i), :], NT_DIMS,
                             preferred_element_type=F32)
        col_max, offs = None, []
        for j in range(i + 1):
            blk_t = st[rows(j)]
            if j == i:
                blk_t = blk_t + bown_ref[hh]
                off = None
            elif j == i - 1:
                blk_t = blk_t + bprev_ref[hh]
                off = sel_neg[r, hh][j:j + 1, rows(i)]
            else:
                off = sel_neg[r, hh][j:j + 1, rows(i)] + far_bias[hh]
            s_refs[slot][rows(j), :] = blk_t
            cand = jnp.max(blk_t, axis=0, keepdims=True)
            cand = cand if off is None else cand + off
            col_max = cand if col_max is None else jnp.maximum(col_max, cand)
            offs.append(off)
        shifts[i, r, hh] = [col_max if off is None else col_max - off for off in offs]

    def pass_b(i, r, hh):
        hi = (i + 1) * MOBA_BLOCK
        slot = slot_of[i, r, hh]
        shift = shifts.pop((i, r, hh))
        for j in range(i + 1):
            p_refs[slot][rows(j), :] = jnp.exp2(s_refs[slot][rows(j), :] - shift[j]).astype(BF16)
        o_aug = jnp.dot(vaug_ref[r * HEADS_PER_STEP + hh, :, :hi], p_refs[slot][:hi, :],
                        preferred_element_type=F32)
        outs[hh] = o_aug[:HEAD_DIM] / o_aug[HEAD_DIM:HEAD_DIM + 1]
        if hh == HEADS_PER_STEP - 1:
            o_pair = [outs.pop(h2) for h2 in range(HEADS_PER_STEP)]
            o_ref[r, rows(i), :] = jnp.concatenate(o_pair, axis=0).T.astype(o_ref.dtype)

    for c in range(len(chains) + ATTN_SLOTS - 1):
        if c < len(chains):
            pass_a(*chains[c])
        if c >= ATTN_SLOTS - 1:
            pass_b(*chains[c - (ATTN_SLOTS - 1)])


def _attention(tbl_t, qk, vt, bias_own, bias_prev):
    b, s, _ = qk.shape
    n_groups = ATT_W // LANES
    n_rows = math.gcd(b, ATTN_ROWS)
    blk = (HEADS_PER_STEP, MOBA_BLOCK, MOBA_BLOCK)
    return pl.pallas_call(
        _attn_kernel,
        grid=(n_groups, b // n_rows),
        in_specs=[pl.BlockSpec(memory_space=pltpu.SMEM),
                  pl.BlockSpec((n_rows, s, LANES), lambda g, i: (i, 0, g)),
                  pl.BlockSpec((n_rows, s, LANES), lambda g, i: (i, 0, n_groups + g)),
                  pl.BlockSpec((n_rows, LANES, s), lambda g, i: (i, g, 0)),
                  pl.BlockSpec(blk, lambda g, i: (g, 0, 0)),
                  pl.BlockSpec(blk, lambda g, i: (g, 0, 0))],
        out_specs=pl.BlockSpec((n_rows, s, LANES), lambda g, i: (i, 0, g)),
        out_shape=jax.ShapeDtypeStruct((b, s, ATT_W), BF16),
        scratch_shapes=[pltpu.VMEM((n_rows * HEADS_PER_STEP, s, LANES), BF16),
                        pltpu.VMEM((n_rows * HEADS_PER_STEP, HEAD_DIM + ONES_ROWS, s), BF16),
                        *[pltpu.VMEM((s, MOBA_BLOCK), F32)] * ATTN_SLOTS,
                        *[pltpu.VMEM((s, MOBA_BLOCK), BF16)] * ATTN_SLOTS],
        compiler_params=pltpu.CompilerParams(
            dimension_semantics=("arbitrary", "arbitrary"), vmem_limit_bytes=VMEM_LIMIT),
        name="moba_attn",
    )(tbl_t, qk, qk, vt, bias_own, bias_prev)


def _conv_kernel(rows_per_seq, cb_ref, a_ref, taps_ref, o_ref, y_ref):
    g = pl.program_id(1)
    nseq, ct, _, _ = a_ref.shape
    rt = nseq * rows_per_seq
    first_blk = lax.broadcasted_iota(jnp.int32, (rt, 1), 0) % rows_per_seq == 0
    for cl in range(ct):
        c = g * ct + cl
        taps = jnp.broadcast_to(taps_ref[cl:cl + 1, :], (CONV_BLK, 2 * CONV_BLK))
        w_c = pltpu.roll(taps, 0, 1, stride=1, stride_axis=0).astype(BF16)
        a_c = a_ref[:, cl].reshape(rt, CONV_BLK)
        res = jnp.dot(a_c, w_c, preferred_element_type=F32)
        carry = pltpu.roll(res[:, CONV_BLK:], 1, 0)
        y = res[:, :CONV_BLK] + jnp.where(first_blk, 0.0, carry) + cb_ref[c]
        for r8 in range(0, rt, SUBLANES):
            y_ref[pl.ds(r8 * CONV_ROW_PITCH + c, SUBLANES, stride=CONV_ROW_PITCH), :] = y[r8:r8 + SUBLANES]

    @pl.when(g == pl.num_programs(1) - 1)
    def _():
        for r in range(rt):
            x = y_ref[r * CONV_ROW_PITCH:r * CONV_ROW_PITCH + CONV_CH, :]
            o_ref[r * CONV_BLK:(r + 1) * CONV_BLK, :] = x.T.astype(o_ref.dtype)


def _conv_module(ut, conv_w, conv_b):
    b, c, s = ut.shape
    rows_per_seq = s // CONV_BLK
    n_rows = b * rows_per_seq
    row_tile = math.gcd(n_rows, CONV_ROW_TILE)
    assert row_tile % rows_per_seq == 0
    a = ut.reshape(b, c, rows_per_seq, CONV_BLK)
    taps = jnp.pad(conv_w[::-1].T.astype(F32), ((0, 0), (0, 2 * CONV_BLK - CONV_WIDTH)))
    return pl.pallas_call(
        functools.partial(_conv_kernel, rows_per_seq),
        grid=(n_rows // row_tile, c // CONV_CH_TILE),
        in_specs=[pl.BlockSpec(memory_space=pltpu.SMEM),
                  pl.BlockSpec((row_tile // rows_per_seq, CONV_CH_TILE, rows_per_seq, CONV_BLK),
                               lambda i, g: (i, g, 0, 0)),
                  pl.BlockSpec((CONV_CH_TILE, 2 * CONV_BLK), lambda i, g: (g, 0))],
        out_specs=pl.BlockSpec((row_tile * CONV_BLK, c), lambda i, g: (i, 0)),
        out_shape=jax.ShapeDtypeStruct((b * s, c), BF16),
        scratch_shapes=[pltpu.VMEM((row_tile * CONV_ROW_PITCH, CONV_BLK), F32)],
        compiler_params=pltpu.CompilerParams(
            dimension_semantics=("arbitrary", "arbitrary"), vmem_limit_bytes=VMEM_LIMIT),
        name="conv_module",
    )(conv_b.astype(F32), a, taps)


def _mix_ffn_kernel(alpha, x_ref, att_ref, cv_ref, gate_ref, p_ref,
                    cg_ref, cb_ref, wa_ref, wc_ref, wm_ref, g1_ref, b1_ref,
                    wg_ref, wu_ref, wd_ref, wpg_ref, bpg_ref, wp_ref, g2_ref, b2_ref, o_ref):
    tm, d = x_ref.shape
    unit = tm // sum(MIX_ROW_FRACTIONS)
    bounds = np.cumsum((0,) + MIX_ROW_FRACTIONS) * unit
    groups = [slice(int(lo), int(hi)) for lo, hi in zip(bounds[:-1], bounds[1:])]
    dot = functools.partial(jnp.dot, preferred_element_type=F32)
    y_att = [dot(att_ref[rs, :], wa_ref[...]) for rs in groups]
    cvn = [_layer_norm(cv_ref[rs, :].astype(F32), cg_ref[...], cb_ref[...]) for rs in groups]
    cvn = [(v * jax.nn.sigmoid(v)).astype(BF16) for v in cvn]
    y_conv = [dot(v, wc_ref[...]) for v in cvn]
    merged = [(gate_ref[rs, :d].astype(F32) * ya + gate_ref[rs, d:].astype(F32) * yc).astype(BF16)
              for rs, ya, yc in zip(groups, y_att, y_conv)]
    mixed = [dot(m, wm_ref[...]) for m in merged]
    pe = [dot(p_ref[rs, :].astype(BF16), wp_ref[...]) for rs in groups]
    x1 = [_layer_norm(alpha * x_ref[rs, :] + mx, g1_ref[...], b1_ref[...]) for rs, mx in zip(groups, mixed)]
    xb = [v.astype(BF16) for v in x1]
    hg = [dot(v, wg_ref[...]) for v in xb]
    hu = [dot(v, wu_ref[...]) for v in xb]
    pg = [dot(v, wpg_ref[...]) for v in xb]
    hid = [(g * jax.nn.sigmoid(g) * u).astype(BF16) for g, u in zip(hg, hu)]
    ffn = [dot(v, wd_ref[...]) for v in hid]
    for h, rs in enumerate(groups):
        ple = jax.nn.sigmoid(pg[h] + bpg_ref[...]) * pe[h]
        o_ref[rs, :] = _layer_norm(alpha * x1[h] + ffn[h] + ple, g2_ref[...], b2_ref[...])


def _mix_ffn(alpha, x, att, cv, gates, p, weights, tm):
    t, d = x.shape
    row = lambda n: pl.BlockSpec((tm, n), lambda i: (i, 0))
    return pl.pallas_call(
        functools.partial(_mix_ffn_kernel, alpha),
        grid=(t // tm,),
        in_specs=[row(d), row(ATT_W), row(CONV_CH), row(2 * d), row(PLE_DIM)]
                 + [_resident(w.shape) for w in weights],
        out_specs=row(d),
        out_shape=jax.ShapeDtypeStruct((t, d), F32),
        compiler_params=pltpu.CompilerParams(
            dimension_semantics=("arbitrary",), vmem_limit_bytes=VMEM_LIMIT),
        name="mix_ffn",
    )(x, att, cv, gates, p, *weights)


def kernel(x, p, w_in, b_gate, bias_table, w_att_out, conv_w, conv_b, conv_ln_g, conv_ln_b,
           w_conv_out, w_mix_out, ln_mix_g, ln_mix_b, w_ffn_gate, w_ffn_up, w_ffn_down,
           w_ple, w_ple_gate, b_ple_gate, ln_ffn_g, ln_ffn_b):
    b, s, d = x.shape
    depth = w_in.shape[0]
    assert d == D_MODEL and s % MOBA_BLOCK == 0
    alpha = (2.0 * depth) ** 0.25
    tm = 1024
    tm_ffn = 512
    row = lambda v: v.reshape(1, -1).astype(F32)
    k_end, v_end = 2 * ATT_W, 3 * ATT_W
    u_end = v_end + 2 * CONV_CH

    tbl_t = bias_table.T.astype(F32)
    bias_own, bias_prev = _bias_tables(tbl_t)
    for i in range(depth):
        w = w_in[i].astype(BF16)
        qk, vt, ut, gates = _proj(x, w[:, :k_end], w[:, k_end:v_end].T, w[:, v_end:u_end].T,
                                  w[:, u_end:], row(b_gate[i]), tm)
        att = _attention(tbl_t, qk, vt, bias_own, bias_prev)
        cv = _conv_module(ut, conv_w[i], conv_b[i])
        weights = (row(conv_ln_g[i]), row(conv_ln_b[i]), w_att_out[i].astype(BF16), w_conv_out[i].astype(BF16), w_mix_out[i].astype(BF16),
                   row(ln_mix_g[i]), row(ln_mix_b[i]),
                   w_ffn_gate[i].astype(BF16), w_ffn_up[i].astype(BF16), w_ffn_down[i].astype(BF16),
                   w_ple_gate[i].astype(BF16), row(b_ple_gate[i]), w_ple[i].astype(BF16),
                   row(ln_ffn_g[i]), row(ln_ffn_b[i]))
        x2 = _mix_ffn(alpha, x.reshape(b * s, d), att.reshape(b * s, ATT_W), cv,
                      gates.reshape(b * s, 2 * d), p[i].reshape(b * s, PLE_DIM), weights, tm_ffn)
        x = x2.reshape(b, s, d)
    return x
```

```python
import functools
import math

import numpy as np
import jax
import jax.numpy as jnp
from jax import lax
from jax.experimental import pallas as pl
from jax.experimental.pallas import tpu as pltpu

D_MODEL = 1024
PLE_DIM = 256
ATT_HEADS = 8
HEAD_DIM = 64
ATT_W = ATT_HEADS * HEAD_DIM
MOBA_BLOCK = 256
MOBA_TOPK = 3
CONV_CH = 512
CONV_WIDTH = 31
NUM_BUCKETS = 32
MAX_DISTANCE = 128
FFN_HIDDEN = 2816
LN_EPS = 1e-5
NEG_INF = -1e30

LANES = 128
SUBLANES = 8
HEADS_PER_STEP = LANES // HEAD_DIM
ONES_ROWS = 16
LOG2E = math.log2(math.e)
ATTN_SLOTS = 4
ATTN_ROWS = 2
PROJ_ROW_SPLIT = 1
MIX_ROW_FRACTIONS = (1, 1)
CONV_BLK = LANES
CONV_ROW_TILE = 64
CONV_CH_TILE = 64
CONV_ROW_PITCH = CONV_CH + 8
VMEM_LIMIT = 56 * 1024 * 1024

BF16 = jnp.bfloat16
F32 = jnp.float32
NT_DIMS = (((1,), (1,)), ((), ()))


def _resident(shape):
    return pl.BlockSpec(shape, lambda *_: (0,) * len(shape), pipeline_mode=pl.Buffered(1))


def _layer_norm(h, g, b):
    mu = jnp.mean(h, axis=-1, keepdims=True)
    d = h - mu
    var = jnp.mean(d * d, axis=-1, keepdims=True)
    return d * lax.rsqrt(var + LN_EPS) * g + b


def _t5_bucket_np(rel):
    n = np.maximum(rel, 0)
    max_exact = NUM_BUCKETS // 2
    nf = np.maximum(n, 1).astype(np.float32)
    large = max_exact + (np.log(nf / np.float32(max_exact)) / np.float32(math.log(MAX_DISTANCE / max_exact))
                         * np.float32(NUM_BUCKETS - max_exact)).astype(np.int32)
    large = np.minimum(large, NUM_BUCKETS - 1)
    return np.where(n < max_exact, n, large).astype(np.int32)


def _bias_bucket_tables():
    kpos = np.arange(MOBA_BLOCK)[:, None]
    qpos = np.arange(MOBA_BLOCK)[None, :]
    rel_own = qpos - kpos
    own = np.where(rel_own >= 0, _t5_bucket_np(rel_own), NUM_BUCKETS)
    prev = _t5_bucket_np(rel_own + MOBA_BLOCK)
    return own.astype(np.int32), prev.astype(np.int32)


def _bias_kernel(tbl_ref, own_ix_ref, prev_ix_ref, own_ref, prev_ref):
    h = pl.program_id(0)
    own_ix = own_ix_ref[...]
    prev_ix = prev_ix_ref[...]
    own = jnp.full(own_ix.shape, NEG_INF, F32)
    prev = jnp.zeros(prev_ix.shape, F32)
    for b in range(NUM_BUCKETS):
        t = tbl_ref[h, b] * LOG2E
        own = jnp.where(own_ix == b, t, own)
        prev = jnp.where(prev_ix == b, t, prev)
    own_ref[0] = own
    prev_ref[0] = prev


def _bias_tables(tbl_t):
    own_ix, prev_ix = _bias_bucket_tables()
    blk = (MOBA_BLOCK, MOBA_BLOCK)
    return pl.pallas_call(
        _bias_kernel,
        grid=(ATT_HEADS,),
        in_specs=[pl.BlockSpec(memory_space=pltpu.SMEM),
                  pl.BlockSpec(blk, lambda h: (0, 0)),
                  pl.BlockSpec(blk, lambda h: (0, 0))],
        out_specs=[pl.BlockSpec((1,) + blk, lambda h: (h, 0, 0)),
                   pl.BlockSpec((1,) + blk, lambda h: (h, 0, 0))],
        out_shape=[jax.ShapeDtypeStruct((ATT_HEADS,) + blk, F32)] * 2,
        name="t5_bias",
    )(tbl_t, jnp.asarray(own_ix), jnp.asarray(prev_ix))


def _proj_kernel(x_ref, wqk_ref, wvt_ref, wagt_ref, wgate_ref, bgate_ref,
                 qk_ref, vt_ref, ut_ref, gate_ref):
    tm = x_ref.shape[1]
    groups = [slice(h * (tm // PROJ_ROW_SPLIT), (h + 1) * (tm // PROJ_ROW_SPLIT)) for h in range(PROJ_ROW_SPLIT)]
    xb = [x_ref[0, rs, :].astype(BF16) for rs in groups]
    for rs, v in zip(groups, xb):
        gl = jnp.dot(v, wgate_ref[...], preferred_element_type=F32) + bgate_ref[...]
        gate_ref[0, rs, :] = jax.nn.sigmoid(gl).astype(BF16)
    for rs, v in zip(groups, xb):
        agt = lax.dot_general(wagt_ref[...], v, NT_DIMS, preferred_element_type=F32)
        ut_ref[0, :, rs] = (agt[:CONV_CH] * jax.nn.sigmoid(agt[CONV_CH:])).astype(BF16)
    for rs, v in zip(groups, xb):
        qk = jnp.dot(v, wqk_ref[...], preferred_element_type=F32)
        n_groups = ATT_W // LANES
        for g in range(2 * n_groups):
            cols = qk[:, g * LANES:(g + 1) * LANES]
            qk_ref[0, g, rs, :] = (cols * (LOG2E * HEAD_DIM ** -0.5) if g < n_groups else cols).astype(BF16)
    for rs, v in zip(groups, xb):
        vt = lax.dot_general(wvt_ref[...], v, NT_DIMS, preferred_element_type=F32)
        vt_ref[0, :, rs] = vt.astype(BF16)


def _proj(x, wqk, wvt, wagt, wgate, bgate, tm):
    b, s, d = x.shape
    return pl.pallas_call(
        _proj_kernel,
        grid=(b, s // tm),
        in_specs=[pl.BlockSpec((1, tm, d), lambda i, j: (i, j, 0)),
                  _resident(wqk.shape), _resident(wvt.shape), _resident(wagt.shape),
                  _resident(wgate.shape), _resident(bgate.shape)],
        out_specs=[pl.BlockSpec((1, 2 * ATT_W // LANES, tm, LANES), lambda i, j: (i, 0, j, 0)),
                   pl.BlockSpec((1, ATT_W, tm), lambda i, j: (i, 0, j)),
                   pl.BlockSpec((1, CONV_CH, tm), lambda i, j: (i, 0, j)),
                   pl.BlockSpec((1, tm, 2 * d), lambda i, j: (i, j, 0))],
        out_shape=[jax.ShapeDtypeStruct((b, 2 * ATT_W // LANES, s, LANES), BF16),
                   jax.ShapeDtypeStruct((b, ATT_W, s), BF16),
                   jax.ShapeDtypeStruct((b, CONV_CH, s), BF16),
                   jax.ShapeDtypeStruct((b, s, 2 * d), BF16)],
        compiler_params=pltpu.CompilerParams(
            dimension_semantics=("arbitrary", "arbitrary"), vmem_limit_bytes=VMEM_LIMIT),
        name="in_proj",
    )(x, wqk, wvt, wagt, wgate, bgate)


def _attn_kernel(tbl_ref, q_ref, k_ref, vt_ref, bown_ref, bprev_ref, o_ref,
                 qm_ref, vaug_ref, *sp_refs):
    s_refs, p_refs = sp_refs[:ATTN_SLOTS], sp_refs[ATTN_SLOTS:]
    hp = pl.program_id(0)
    n_rows, _, s, _ = q_ref.shape
    nb = s // MOBA_BLOCK
    lane = lax.broadcasted_iota(jnp.int32, (1, LANES), 1)
    q_blk = lax.broadcasted_iota(jnp.int32, (1, s), 1) // MOBA_BLOCK
    blk = lax.broadcasted_iota(jnp.int32, (nb, 1), 0)
    past = blk < q_blk
    far_bias = [tbl_ref[hp * HEADS_PER_STEP + hh, NUM_BUCKETS - 1] * LOG2E for hh in range(HEADS_PER_STEP)]

    sel_neg = {}
    for r in range(n_rows):
        q = q_ref[r, 0]
        k = k_ref[r, 0]
        k_mean = (jnp.sum(k.astype(F32).reshape(nb, MOBA_BLOCK, LANES), axis=1)
                  * (1.0 / MOBA_BLOCK)).astype(BF16)
        for hh in range(HEADS_PER_STEP):
            rh = r * HEADS_PER_STEP + hh
            qm = jnp.where(lane // HEAD_DIM == hh, q, jnp.zeros_like(q))
            qm_ref[rh] = qm
            vaug_ref[rh, :HEAD_DIM, :] = vt_ref[r, hh * HEAD_DIM:(hh + 1) * HEAD_DIM, :]
            vaug_ref[rh, HEAD_DIM:, :] = jnp.ones((ONES_ROWS, s), BF16)
            gate = lax.dot_general(k_mean, qm, NT_DIMS, preferred_element_type=F32)
            rank = jnp.zeros((nb, s), jnp.int32)
            for m in range(nb):
                gm = gate[m:m + 1, :]
                beats = (gm > gate) | ((gm == gate) & (m < blk))
                rank = rank + jnp.where(beats & (m < q_blk), 1, 0)
            sel_neg[r, hh] = jnp.where(past & (rank < MOBA_TOPK), 0.0, NEG_INF).astype(F32)

    rows = lambda j: slice(j * MOBA_BLOCK, (j + 1) * MOBA_BLOCK)
    chains = [(i, r, hh) for i in range(nb) for r in range(n_rows) for hh in range(HEADS_PER_STEP)]
    slot_of = {chain: c % ATTN_SLOTS for c, chain in enumerate(chains)}
    shifts = {}
    outs = {}

    def pass_a(i, r, hh):
        hi = (i + 1) * MOBA_BLOCK
        slot = slot_of[i, r, hh]
        st = lax.dot_general(k_ref[r, 0, :hi, :], qm_ref[r * HEADS_PER_STEP + hh, rows(i), :], NT_DIMS,
                             preferred_element_type=F32)
        col_max, offs = None, []
        for j in range(i + 1):
            blk_t = st[rows(j)]
            if j == i:
                blk_t = blk_t + bown_ref[hh]
                off = None
            elif j == i - 1:
                blk_t = blk_t + bprev_ref[hh]
                off = sel_neg[r, hh][j:j + 1, rows(i)]
            else:
                off = sel_neg[r, hh][j:j + 1, rows(i)] + far_bias[hh]
            s_refs[slot][rows(j), :] = blk_t
            cand = jnp.max(blk_t, axis=0, keepdims=True)
            cand = cand if off is None else cand + off
            col_max = cand if col_max is None else jnp.maximum(col_max, cand)
            offs.append(off)
        shifts[i, r, hh] = [col_max if off is None else col_max - off for off in offs]

    def pass_b(i, r, hh):
        hi = (i + 1) * MOBA_BLOCK
        slot = slot_of[i, r, hh]
        shift = shifts.pop((i, r, hh))
        for j in range(i + 1):
            p_refs[slot][rows(j), :] = jnp.exp2(s_refs[slot][rows(j), :] - shift[j]).astype(BF16)
        o_aug = jnp.dot(vaug_ref[r * HEADS_PER_STEP + hh, :, :hi], p_refs[slot][:hi, :],
                        preferred_element_type=F32)
        outs[hh] = o_aug[:HEAD_DIM] / o_aug[HEAD_DIM:HEAD_DIM + 1]
        if hh == HEADS_PER_STEP - 1:
            o_pair = [outs.pop(h2) for h2 in range(HEADS_PER_STEP)]
            o_ref[r, 0, rows(i), :] = jnp.concatenate(o_pair, axis=0).T.astype(o_ref.dtype)

    for c in range(len(chains) + ATTN_SLOTS - 1):
        if c < len(chains):
            pass_a(*chains[c])
        if c >= ATTN_SLOTS - 1:
            pass_b(*chains[c - (ATTN_SLOTS - 1)])


def _attention(tbl_t, qk, vt, bias_own, bias_prev):
    b, _, s, _ = qk.shape
    n_groups = ATT_W // LANES
    n_rows = math.gcd(b, ATTN_ROWS)
    blk = (HEADS_PER_STEP, MOBA_BLOCK, MOBA_BLOCK)
    return pl.pallas_call(
        _attn_kernel,
        grid=(n_groups, b // n_rows),
        in_specs=[pl.BlockSpec(memory_space=pltpu.SMEM),
                  pl.BlockSpec((n_rows, 1, s, LANES), lambda g, i: (i, g, 0, 0)),
                  pl.BlockSpec((n_rows, 1, s, LANES), lambda g, i: (i, n_groups + g, 0, 0)),
                  pl.BlockSpec((n_rows, LANES, s), lambda g, i: (i, g, 0)),
                  pl.BlockSpec(blk, lambda g, i: (g, 0, 0)),
                  pl.BlockSpec(blk, lambda g, i: (g, 0, 0))],
        out_specs=pl.BlockSpec((n_rows, 1, s, LANES), lambda g, i: (i, g, 0, 0)),
        out_shape=jax.ShapeDtypeStruct((b, n_groups, s, LANES), BF16),
        scratch_shapes=[pltpu.VMEM((n_rows * HEADS_PER_STEP, s, LANES), BF16),
                        pltpu.VMEM((n_rows * HEADS_PER_STEP, HEAD_DIM + ONES_ROWS, s), BF16),
                        *[pltpu.VMEM((s, MOBA_BLOCK), F32)] * ATTN_SLOTS,
                        *[pltpu.VMEM((s, MOBA_BLOCK), BF16)] * ATTN_SLOTS],
        compiler_params=pltpu.CompilerParams(
            dimension_semantics=("arbitrary", "arbitrary"), vmem_limit_bytes=VMEM_LIMIT),
        name="moba_attn",
    )(tbl_t, qk, qk, vt, bias_own, bias_prev)


def _conv_kernel(rows_per_seq, cb_ref, a_ref, taps_ref, o_ref, y_ref):
    g = pl.program_id(1)
    nseq, ct, _, _ = a_ref.shape
    rt = nseq * rows_per_seq
    first_blk = lax.broadcasted_iota(jnp.int32, (rt, 1), 0) % rows_per_seq == 0
    for cl in range(ct):
        c = g * ct + cl
        taps = jnp.broadcast_to(taps_ref[cl:cl + 1, :], (CONV_BLK, 2 * CONV_BLK))
        w_c = pltpu.roll(taps, 0, 1, stride=1, stride_axis=0).astype(BF16)
        a_c = a_ref[:, cl].reshape(rt, CONV_BLK)
        res = jnp.dot(a_c, w_c, preferred_element_type=F32)
        carry = pltpu.roll(res[:, CONV_BLK:], 1, 0)
        y = res[:, :CONV_BLK] + jnp.where(first_blk, 0.0, carry) + cb_ref[c]
        for r8 in range(0, rt, SUBLANES):
            y_ref[pl.ds(r8 * CONV_ROW_PITCH + c, SUBLANES, stride=CONV_ROW_PITCH), :] = y[r8:r8 + SUBLANES]

    @pl.when(g == pl.num_programs(1) - 1)
    def _():
        for r in range(rt):
            x = y_ref[r * CONV_ROW_PITCH:r * CONV_ROW_PITCH + CONV_CH, :]
            o_ref[r * CONV_BLK:(r + 1) * CONV_BLK, :] = x.T.astype(o_ref.dtype)


def _conv_module(ut, conv_w, conv_b):
    b, c, s = ut.shape
    rows_per_seq = s // CONV_BLK
    n_rows = b * rows_per_seq
    row_tile = math.gcd(n_rows, CONV_ROW_TILE)
    assert row_tile % rows_per_seq == 0
    a = ut.reshape(b, c, rows_per_seq, CONV_BLK)
    taps = jnp.pad(conv_w[::-1].T.astype(F32), ((0, 0), (0, 2 * CONV_BLK - CONV_WIDTH)))
    return pl.pallas_call(
        functools.partial(_conv_kernel, rows_per_seq),
        grid=(n_rows // row_tile, c // CONV_CH_TILE),
        in_specs=[pl.BlockSpec(memory_space=pltpu.SMEM),
                  pl.BlockSpec((row_tile // rows_per_seq, CONV_CH_TILE, rows_per_seq, CONV_BLK),
                               lambda i, g: (i, g, 0, 0)),
                  pl.BlockSpec((CONV_CH_TILE, 2 * CONV_BLK), lambda i, g: (g, 0))],
        out_specs=pl.BlockSpec((row_tile * CONV_BLK, c), lambda i, g: (i, 0)),
        out_shape=jax.ShapeDtypeStruct((b * s, c), BF16),
        scratch_shapes=[pltpu.VMEM((row_tile * CONV_ROW_PITCH, CONV_BLK), F32)],
        compiler_params=pltpu.CompilerParams(
            dimension_semantics=("arbitrary", "arbitrary"), vmem_limit_bytes=VMEM_LIMIT),
        name="conv_module",
    )(conv_b.astype(F32), a, taps)


def _mix_ffn_kernel(alpha, x_ref, att_ref, cv_ref, gate_ref, p_ref,
                    cg_ref, cb_ref, wa_ref, wc_ref, wm_ref, g1_ref, b1_ref,
                    wg_ref, wu_ref, wd_ref, wpg_ref, bpg_ref, wp_ref, g2_ref, b2_ref, o_ref):
    tm, d = x_ref.shape
    unit = tm // sum(MIX_ROW_FRACTIONS)
    bounds = np.cumsum((0,) + MIX_ROW_FRACTIONS) * unit
    groups = [slice(int(lo), int(hi)) for lo, hi in zip(bounds[:-1], bounds[1:])]
    dot = functools.partial(jnp.dot, preferred_element_type=F32)
    att = [jnp.concatenate([att_ref[0, g, rs, :] for g in range(att_ref.shape[1])], axis=-1) for rs in groups]
    y_att = [dot(v, wa_ref[...]) for v in att]
    cvn = [_layer_norm(cv_ref[rs, :].astype(F32), cg_ref[...], cb_ref[...]) for rs in groups]
    cvn = [(v * jax.nn.sigmoid(v)).astype(BF16) for v in cvn]
    y_conv = [dot(v, wc_ref[...]) for v in cvn]
    merged = [(gate_ref[rs, :d].astype(F32) * ya + gate_ref[rs, d:].astype(F32) * yc).astype(BF16)
              for rs, ya, yc in zip(groups, y_att, y_conv)]
    mixed = [dot(m, wm_ref[...]) for m in merged]
    pe = [dot(p_ref[rs, :].astype(BF16), wp_ref[...]) for rs in groups]
    x1 = [_layer_norm(alpha * x_ref[rs, :] + mx, g1_ref[...], b1_ref[...]) for rs, mx in zip(groups, mixed)]
    xb = [v.astype(BF16) for v in x1]
    hg = [dot(v, wg_ref[...]) for v in xb]
    hu = [dot(v, wu_ref[...]) for v in xb]
    pg = [dot(v, wpg_ref[...]) for v in xb]
    hid = [(g * jax.nn.sigmoid(g) * u).astype(BF16) for g, u in zip(hg, hu)]
    ffn = [dot(v, wd_ref[...]) for v in hid]
    for h, rs in enumerate(groups):
        ple = jax.nn.sigmoid(pg[h] + bpg_ref[...]) * pe[h]
        o_ref[rs, :] = _layer_norm(alpha * x1[h] + ffn[h] + ple, g2_ref[...], b2_ref[...])


def _mix_ffn(alpha, x, att, cv, gates, p, weights, tm):
    t, d = x.shape
    n_groups, s = att.shape[1:3]
    tiles_per_seq = s // tm
    row = lambda n: pl.BlockSpec((tm, n), lambda i: (i, 0))
    att_spec = pl.BlockSpec((1, n_groups, tm, LANES), lambda i: (i // tiles_per_seq, 0, i % tiles_per_seq, 0))
    return pl.pallas_call(
        functools.partial(_mix_ffn_kernel, alpha),
        grid=(t // tm,),
        in_specs=[row(d), att_spec, row(CONV_CH), row(2 * d), row(PLE_DIM)]
                 + [_resident(w.shape) for w in weights],
        out_specs=row(d),
        out_shape=jax.ShapeDtypeStruct((t, d), F32),
        compiler_params=pltpu.CompilerParams(
            dimension_semantics=("arbitrary",), vmem_limit_bytes=VMEM_LIMIT),
        name="mix_ffn",
    )(x, att, cv, gates, p, *weights)


def kernel(x, p, w_in, b_gate, bias_table, w_att_out, conv_w, conv_b, conv_ln_g, conv_ln_b,
           w_conv_out, w_mix_out, ln_mix_g, ln_mix_b, w_ffn_gate, w_ffn_up, w_ffn_down,
           w_ple, w_ple_gate, b_ple_gate, ln_ffn_g, ln_ffn_b):
    b, s, d = x.shape
    depth = w_in.shape[0]
    assert d == D_MODEL and s % MOBA_BLOCK == 0
    alpha = (2.0 * depth) ** 0.25
    tm = 1024
    tm_ffn = 512
    row = lambda v: v.reshape(1, -1).astype(F32)
    k_end, v_end = 2 * ATT_W, 3 * ATT_W
    u_end = v_end + 2 * CONV_CH

    tbl_t = bias_table.T.astype(F32)
    bias_own, bias_prev = _bias_tables(tbl_t)
    for i in range(depth):
        w = w_in[i].astype(BF16)
        qk, vt, ut, gates = _proj(x, w[:, :k_end], w[:, k_end:v_end].T, w[:, v_end:u_end].T,
                                  w[:, u_end:], row(b_gate[i]), tm)
        att = _attention(tbl_t, qk, vt, bias_own, bias_prev)
        cv = _conv_module(ut, conv_w[i], conv_b[i])
        weights = (row(conv_ln_g[i]), row(conv_ln_b[i]), w_att_out[i].astype(BF16), w_conv_out[i].astype(BF16), w_mix_out[i].astype(BF16),
                   row(ln_mix_g[i]), row(ln_mix_b[i]),
                   w_ffn_gate[i].astype(BF16), w_ffn_up[i].astype(BF16), w_ffn_down[i].astype(BF16),
                   w_ple_gate[i].astype(BF16), row(b_ple_gate[i]), w_ple[i].astype(BF16),
                   row(ln_ffn_g[i]), row(ln_ffn_b[i]))
        x2 = _mix_ffn(alpha, x.reshape(b * s, d), att, cv,
                      gates.reshape(b * s, 2 * d), p[i].reshape(b * s, PLE_DIM), weights, tm_ffn)
        x = x2.reshape(b, s, d)
    return x
```

```python
import functools
import math

import numpy as np
import jax
import jax.numpy as jnp
from jax import lax
from jax.experimental import pallas as pl
from jax.experimental.pallas import tpu as pltpu

D_MODEL = 1024
PLE_DIM = 256
ATT_HEADS = 8
HEAD_DIM = 64
ATT_W = ATT_HEADS * HEAD_DIM
MOBA_BLOCK = 256
MOBA_TOPK = 3
CONV_CH = 512
CONV_WIDTH = 31
NUM_BUCKETS = 32
MAX_DISTANCE = 128
FFN_HIDDEN = 2816
LN_EPS = 1e-5
NEG_INF = -1e30

LANES = 128
SUBLANES = 8
HEADS_PER_STEP = LANES // HEAD_DIM
ONES_ROWS = 16
LOG2E = math.log2(math.e)
ATTN_SLOTS = 4
ATTN_ROWS = 2
PROJ_ROWS = 1024
MIX_ROWS = 512
MIX_ROW_GROUPS = 2
CONV_BLK = LANES
CONV_ROW_TILE = 64
CONV_CH_TILE = 64
CONV_ROW_PITCH = CONV_CH + 8
VMEM_LIMIT = 56 * 1024 * 1024

BF16 = jnp.bfloat16
F32 = jnp.float32
NT_DIMS = (((1,), (1,)), ((), ()))


def _resident(shape):
    return pl.BlockSpec(shape, lambda *_: (0,) * len(shape), pipeline_mode=pl.Buffered(1))


def _layer_norm(h, g, b):
    mu = jnp.mean(h, axis=-1, keepdims=True)
    d = h - mu
    var = jnp.mean(d * d, axis=-1, keepdims=True)
    return d * lax.rsqrt(var + LN_EPS) * g + b


def _t5_bucket_np(rel):
    n = np.maximum(rel, 0)
    max_exact = NUM_BUCKETS // 2
    nf = np.maximum(n, 1).astype(np.float32)
    large = max_exact + (np.log(nf / np.float32(max_exact)) / np.float32(math.log(MAX_DISTANCE / max_exact))
                         * np.float32(NUM_BUCKETS - max_exact)).astype(np.int32)
    large = np.minimum(large, NUM_BUCKETS - 1)
    return np.where(n < max_exact, n, large).astype(np.int32)


def _bias_bucket_tables():
    kpos = np.arange(MOBA_BLOCK)[:, None]
    qpos = np.arange(MOBA_BLOCK)[None, :]
    rel_own = qpos - kpos
    own = np.where(rel_own >= 0, _t5_bucket_np(rel_own), NUM_BUCKETS)
    prev = _t5_bucket_np(rel_own + MOBA_BLOCK)
    return own.astype(np.int32), prev.astype(np.int32)


def _bias_kernel(tbl_ref, own_ix_ref, prev_ix_ref, own_ref, prev_ref):
    h = pl.program_id(0)
    own_ix = own_ix_ref[...]
    prev_ix = prev_ix_ref[...]
    own = jnp.full(own_ix.shape, NEG_INF, F32)
    prev = jnp.zeros(prev_ix.shape, F32)
    for b in range(NUM_BUCKETS):
        t = tbl_ref[h, b] * LOG2E
        own = jnp.where(own_ix == b, t, own)
        prev = jnp.where(prev_ix == b, t, prev)
    own_ref[0] = own
    prev_ref[0] = prev


def _bias_tables(tbl_t):
    own_ix, prev_ix = _bias_bucket_tables()
    blk = (MOBA_BLOCK, MOBA_BLOCK)
    return pl.pallas_call(
        _bias_kernel,
        grid=(ATT_HEADS,),
        in_specs=[pl.BlockSpec(memory_space=pltpu.SMEM),
                  pl.BlockSpec(blk, lambda h: (0, 0)),
                  pl.BlockSpec(blk, lambda h: (0, 0))],
        out_specs=[pl.BlockSpec((1,) + blk, lambda h: (h, 0, 0)),
                   pl.BlockSpec((1,) + blk, lambda h: (h, 0, 0))],
        out_shape=[jax.ShapeDtypeStruct((ATT_HEADS,) + blk, F32)] * 2,
        name="t5_bias",
    )(tbl_t, jnp.asarray(own_ix), jnp.asarray(prev_ix))


def _proj_kernel(x_ref, wqk_ref, wvt_ref, wagt_ref, wgate_ref, bgate_ref,
                 qk_ref, vt_ref, ut_ref, gate_ref):
    xb = x_ref[0].astype(BF16)
    gl = jnp.dot(xb, wgate_ref[...], preferred_element_type=F32) + bgate_ref[...]
    gate_ref[0] = jax.nn.sigmoid(gl).astype(BF16)
    agt = lax.dot_general(wagt_ref[...], xb, NT_DIMS, preferred_element_type=F32)
    ut_ref[0] = (agt[:CONV_CH] * jax.nn.sigmoid(agt[CONV_CH:])).astype(BF16)
    qk = jnp.dot(xb, wqk_ref[...], preferred_element_type=F32)
    qk_ref[0, :, :ATT_W] = (qk[:, :ATT_W] * (LOG2E * HEAD_DIM ** -0.5)).astype(BF16)
    qk_ref[0, :, ATT_W:] = qk[:, ATT_W:].astype(BF16)
    vt = lax.dot_general(wvt_ref[...], xb, NT_DIMS, preferred_element_type=F32)
    vt_ref[0] = vt.astype(BF16)


def _proj(x, wqk, wvt, wagt, wgate, bgate):
    b, s, d = x.shape
    tm = math.gcd(s, PROJ_ROWS)
    return pl.pallas_call(
        _proj_kernel,
        grid=(b, s // tm),
        in_specs=[pl.BlockSpec((1, tm, d), lambda i, j: (i, j, 0)),
                  _resident(wqk.shape), _resident(wvt.shape), _resident(wagt.shape),
                  _resident(wgate.shape), _resident(bgate.shape)],
        out_specs=[pl.BlockSpec((1, tm, 2 * ATT_W), lambda i, j: (i, j, 0)),
                   pl.BlockSpec((1, ATT_W, tm), lambda i, j: (i, 0, j)),
                   pl.BlockSpec((1, CONV_CH, tm), lambda i, j: (i, 0, j)),
                   pl.BlockSpec((1, tm, 2 * d), lambda i, j: (i, j, 0))],
        out_shape=[jax.ShapeDtypeStruct((b, s, 2 * ATT_W), BF16),
                   jax.ShapeDtypeStruct((b, ATT_W, s), BF16),
                   jax.ShapeDtypeStruct((b, CONV_CH, s), BF16),
                   jax.ShapeDtypeStruct((b, s, 2 * d), BF16)],
        compiler_params=pltpu.CompilerParams(
            dimension_semantics=("arbitrary", "arbitrary"), vmem_limit_bytes=VMEM_LIMIT),
        name="in_proj",
    )(x, wqk, wvt, wagt, wgate, bgate)


def _attn_kernel(tbl_ref, q_ref, k_ref, vt_ref, bown_ref, bprev_ref, o_ref,
                 qm_ref, vaug_ref, *sp_refs):
    s_refs, p_refs = sp_refs[:ATTN_SLOTS], sp_refs[ATTN_SLOTS:]
    hp = pl.program_id(0)
    n_rows, s, _ = q_ref.shape
    nb = s // MOBA_BLOCK
    lane = lax.broadcasted_iota(jnp.int32, (1, LANES), 1)
    q_blk = lax.broadcasted_iota(jnp.int32, (1, s), 1) // MOBA_BLOCK
    blk = lax.broadcasted_iota(jnp.int32, (nb, 1), 0)
    past = blk < q_blk
    far_bias = [tbl_ref[hp * HEADS_PER_STEP + hh, NUM_BUCKETS - 1] * LOG2E for hh in range(HEADS_PER_STEP)]

    sel_neg = {}
    for r in range(n_rows):
        q = q_ref[r]
        k = k_ref[r]
        k_mean = (jnp.sum(k.astype(F32).reshape(nb, MOBA_BLOCK, LANES), axis=1)
                  * (1.0 / MOBA_BLOCK)).astype(BF16)
        for hh in range(HEADS_PER_STEP):
            rh = r * HEADS_PER_STEP + hh
            qm = jnp.where(lane // HEAD_DIM == hh, q, jnp.zeros_like(q))
            qm_ref[rh] = qm
            vaug_ref[rh, :HEAD_DIM, :] = vt_ref[r, hh * HEAD_DIM:(hh + 1) * HEAD_DIM, :]
            vaug_ref[rh, HEAD_DIM:, :] = jnp.ones((ONES_ROWS, s), BF16)
            gate = lax.dot_general(k_mean, qm, NT_DIMS, preferred_element_type=F32)
            rank = jnp.zeros((nb, s), jnp.int32)
            for m in range(nb):
                gm = gate[m:m + 1, :]
                beats = (gm > gate) | ((gm == gate) & (m < blk))
                rank = rank + jnp.where(beats & (m < q_blk), 1, 0)
            sel_neg[r, hh] = jnp.where(past & (rank < MOBA_TOPK), 0.0, NEG_INF).astype(F32)

    rows = lambda j: slice(j * MOBA_BLOCK, (j + 1) * MOBA_BLOCK)
    chains = [(i, r, hh) for i in range(nb) for r in range(n_rows) for hh in range(HEADS_PER_STEP)]
    slot_of = {chain: c % ATTN_SLOTS for c, chain in enumerate(chains)}
    shifts = {}
    outs = {}

    def pass_a(i, r, hh):
        hi = (i + 1) * MOBA_BLOCK
        slot = slot_of[i, r, hh]
        st = lax.dot_general(k_ref[r, :hi, :], qm_ref[r * HEADS_PER_STEP + hh, rows(i), :], NT_DIMS,
                             preferred_element_type=F32)
        col_max, offs = None, []
        for j in range(i + 1):
            blk_t = st[rows(j)]
            if j == i:
                blk_t = blk_t + bown_ref[hh]
                off = None
            elif j == i - 1:
                blk_t = blk_t + bprev_ref[hh]
                off = sel_neg[r, hh][j:j + 1, rows(i)]
            else:
                off = sel_neg[r, hh][j:j + 1, rows(i)] + far_bias[hh]
            s_refs[slot][rows(j), :] = blk_t
            cand = jnp.max(blk_t, axis=0, keepdims=True)
            cand = cand if off is None else cand + off
            col_max = cand if col_max is None else jnp.maximum(col_max, cand)
            offs.append(off)
        shifts[i, r, hh] = [col_max if off is None else col_max - off for off in offs]

    def pass_b(i, r, hh):
        hi = (i + 1) * MOBA_BLOCK
        slot = slot_of[i, r, hh]
        shift = shifts.pop((i, r, hh))
        for j in range(i + 1):
            p_refs[slot][rows(j), :] = jnp.exp2(s_refs[slot][rows(j), :] - shift[j]).astype(BF16)
        o_aug = jnp.dot(vaug_ref[r * HEADS_PER_STEP + hh, :, :hi], p_refs[slot][:hi, :],
                        preferred_element_type=F32)
        outs[hh] = o_aug[:HEAD_DIM] / o_aug[HEAD_DIM:HEAD_DIM + 1]
        if hh == HEADS_PER_STEP - 1:
            o_pair = [outs.pop(h2) for h2 in range(HEADS_PER_STEP)]
            o_ref[r, rows(i), :] = jnp.concatenate(o_pair, axis=0).T.astype(o_ref.dtype)

    for c in range(len(chains) + ATTN_SLOTS - 1):
        if c < len(chains):
            pass_a(*chains[c])
        if c >= ATTN_SLOTS - 1:
            pass_b(*chains[c - (ATTN_SLOTS - 1)])


def _attention(tbl_t, qk, vt, bias_own, bias_prev):
    b, s, _ = qk.shape
    n_groups = ATT_W // LANES
    n_rows = math.gcd(b, ATTN_ROWS)
    blk = (HEADS_PER_STEP, MOBA_BLOCK, MOBA_BLOCK)
    return pl.pallas_call(
        _attn_kernel,
        grid=(n_groups, b // n_rows),
        in_specs=[pl.BlockSpec(memory_space=pltpu.SMEM),
                  pl.BlockSpec((n_rows, s, LANES), lambda g, i: (i, 0, g)),
                  pl.BlockSpec((n_rows, s, LANES), lambda g, i: (i, 0, n_groups + g)),
                  pl.BlockSpec((n_rows, LANES, s), lambda g, i: (i, g, 0)),
                  pl.BlockSpec(blk, lambda g, i: (g, 0, 0)),
                  pl.BlockSpec(blk, lambda g, i: (g, 0, 0))],
        out_specs=pl.BlockSpec((n_rows, s, LANES), lambda g, i: (i, 0, g)),
        out_shape=jax.ShapeDtypeStruct((b, s, ATT_W), BF16),
        scratch_shapes=[pltpu.VMEM((n_rows * HEADS_PER_STEP, s, LANES), BF16),
                        pltpu.VMEM((n_rows * HEADS_PER_STEP, HEAD_DIM + ONES_ROWS, s), BF16),
                        *[pltpu.VMEM((s, MOBA_BLOCK), F32)] * ATTN_SLOTS,
                        *[pltpu.VMEM((s, MOBA_BLOCK), BF16)] * ATTN_SLOTS],
        compiler_params=pltpu.CompilerParams(
            dimension_semantics=("arbitrary", "arbitrary"), vmem_limit_bytes=VMEM_LIMIT),
        name="moba_attn",
    )(tbl_t, qk, qk, vt, bias_own, bias_prev)


def _conv_kernel(rows_per_seq, cb_ref, a_ref, taps_ref, o_ref, y_ref):
    g = pl.program_id(1)
    nseq, ct, _, _ = a_ref.shape
    rt = nseq * rows_per_seq
    first_blk = lax.broadcasted_iota(jnp.int32, (rt, 1), 0) % rows_per_seq == 0
    for cl in range(ct):
        c = g * ct + cl
        taps = jnp.broadcast_to(taps_ref[cl:cl + 1, :], (CONV_BLK, 2 * CONV_BLK))
        w_c = pltpu.roll(taps, 0, 1, stride=1, stride_axis=0).astype(BF16)
        a_c = a_ref[:, cl].reshape(rt, CONV_BLK)
        res = jnp.dot(a_c, w_c, preferred_element_type=F32)
        carry = pltpu.roll(res[:, CONV_BLK:], 1, 0)
        y = res[:, :CONV_BLK] + jnp.where(first_blk, 0.0, carry) + cb_ref[c]
        for r8 in range(0, rt, SUBLANES):
            y_ref[pl.ds(r8 * CONV_ROW_PITCH + c, SUBLANES, stride=CONV_ROW_PITCH), :] = y[r8:r8 + SUBLANES]

    @pl.when(g == pl.num_programs(1) - 1)
    def _():
        for r in range(rt):
            x = y_ref[r * CONV_ROW_PITCH:r * CONV_ROW_PITCH + CONV_CH, :]
            o_ref[r * CONV_BLK:(r + 1) * CONV_BLK, :] = x.T.astype(o_ref.dtype)


def _conv_module(ut, conv_w, conv_b):
    b, c, s = ut.shape
    rows_per_seq = s // CONV_BLK
    n_rows = b * rows_per_seq
    row_tile = math.gcd(n_rows, CONV_ROW_TILE)
    assert row_tile % rows_per_seq == 0
    a = ut.reshape(b, c, rows_per_seq, CONV_BLK)
    taps = jnp.pad(conv_w[::-1].T.astype(F32), ((0, 0), (0, 2 * CONV_BLK - CONV_WIDTH)))
    return pl.pallas_call(
        functools.partial(_conv_kernel, rows_per_seq),
        grid=(n_rows // row_tile, c // CONV_CH_TILE),
        in_specs=[pl.BlockSpec(memory_space=pltpu.SMEM),
                  pl.BlockSpec((row_tile // rows_per_seq, CONV_CH_TILE, rows_per_seq, CONV_BLK),
                               lambda i, g: (i, g, 0, 0)),
                  pl.BlockSpec((CONV_CH_TILE, 2 * CONV_BLK), lambda i, g: (g, 0))],
        out_specs=pl.BlockSpec((row_tile * CONV_BLK, c), lambda i, g: (i, 0)),
        out_shape=jax.ShapeDtypeStruct((b * s, c), BF16),
        scratch_shapes=[pltpu.VMEM((row_tile * CONV_ROW_PITCH, CONV_BLK), F32)],
        compiler_params=pltpu.CompilerParams(
            dimension_semantics=("arbitrary", "arbitrary"), vmem_limit_bytes=VMEM_LIMIT),
        name="conv_module",
    )(conv_b.astype(F32), a, taps)


def _mix_ffn_kernel(alpha, x_ref, att_ref, cv_ref, gate_ref, p_ref,
                    cg_ref, cb_ref, wa_ref, wc_ref, wm_ref, g1_ref, b1_ref,
                    wg_ref, wu_ref, wd_ref, wpg_ref, bpg_ref, wp_ref, g2_ref, b2_ref, o_ref):
    tm, d = x_ref.shape
    groups = [slice(h * (tm // MIX_ROW_GROUPS), (h + 1) * (tm // MIX_ROW_GROUPS)) for h in range(MIX_ROW_GROUPS)]
    dot = functools.partial(jnp.dot, preferred_element_type=F32)
    y_att = [dot(att_ref[rs, :], wa_ref[...]) for rs in groups]
    cvn = [_layer_norm(cv_ref[rs, :].astype(F32), cg_ref[...], cb_ref[...]) for rs in groups]
    cvn = [(v * jax.nn.sigmoid(v)).astype(BF16) for v in cvn]
    y_conv = [dot(v, wc_ref[...]) for v in cvn]
    merged = [(gate_ref[rs, :d].astype(F32) * ya + gate_ref[rs, d:].astype(F32) * yc).astype(BF16)
              for rs, ya, yc in zip(groups, y_att, y_conv)]
    mixed = [dot(m, wm_ref[...]) for m in merged]
    pe = [dot(p_ref[rs, :].astype(BF16), wp_ref[...]) for rs in groups]
    x1 = [_layer_norm(alpha * x_ref[rs, :] + mx, g1_ref[...], b1_ref[...]) for rs, mx in zip(groups, mixed)]
    xb = [v.astype(BF16) for v in x1]
    hg = [dot(v, wg_ref[...]) for v in xb]
    hu = [dot(v, wu_ref[...]) for v in xb]
    pg = [dot(v, wpg_ref[...]) for v in xb]
    hid = [(g * jax.nn.sigmoid(g) * u).astype(BF16) for g, u in zip(hg, hu)]
    ffn = [dot(v, wd_ref[...]) for v in hid]
    for h, rs in enumerate(groups):
        ple = jax.nn.sigmoid(pg[h] + bpg_ref[...]) * pe[h]
        o_ref[rs, :] = _layer_norm(alpha * x1[h] + ffn[h] + ple, g2_ref[...], b2_ref[...])


def _mix_ffn(alpha, x, att, cv, gates, p, weights):
    t, d = x.shape
    tm = math.gcd(t, MIX_ROWS)
    row = lambda n: pl.BlockSpec((tm, n), lambda i: (i, 0))
    return pl.pallas_call(
        functools.partial(_mix_ffn_kernel, alpha),
        grid=(t // tm,),
        in_specs=[row(d), row(ATT_W), row(CONV_CH), row(2 * d), row(PLE_DIM)]
                 + [_resident(w.shape) for w in weights],
        out_specs=row(d),
        out_shape=jax.ShapeDtypeStruct((t, d), F32),
        compiler_params=pltpu.CompilerParams(
            dimension_semantics=("arbitrary",), vmem_limit_bytes=VMEM_LIMIT),
        name="mix_ffn",
    )(x, att, cv, gates, p, *weights)


def kernel(x, p, w_in, b_gate, bias_table, w_att_out, conv_w, conv_b, conv_ln_g, conv_ln_b,
           w_conv_out, w_mix_out, ln_mix_g, ln_mix_b, w_ffn_gate, w_ffn_up, w_ffn_down,
           w_ple, w_ple_gate, b_ple_gate, ln_ffn_g, ln_ffn_b):
    b, s, d = x.shape
    depth = w_in.shape[0]
    assert d == D_MODEL and s % MOBA_BLOCK == 0
    alpha = (2.0 * depth) ** 0.25
    row = lambda v: v.reshape(1, -1).astype(F32)
    k_end, v_end = 2 * ATT_W, 3 * ATT_W
    u_end = v_end + 2 * CONV_CH

    tbl_t = bias_table.T.astype(F32)
    bias_own, bias_prev = _bias_tables(tbl_t)
    for i in range(depth):
        w = w_in[i].astype(BF16)
        qk, vt, ut, gates = _proj(x, w[:, :k_end], w[:, k_end:v_end].T, w[:, v_end:u_end].T,
                                  w[:, u_end:], row(b_gate[i]))
        att = _attention(tbl_t, qk, vt, bias_own, bias_prev)
        cv = _conv_module(ut, conv_w[i], conv_b[i])
        weights = (row(conv_ln_g[i]), row(conv_ln_b[i]),
                   w_att_out[i].astype(BF16), w_conv_out[i].astype(BF16), w_mix_out[i].astype(BF16),
                   row(ln_mix_g[i]), row(ln_mix_b[i]),
                   w_ffn_gate[i].astype(BF16), w_ffn_up[i].astype(BF16), w_ffn_down[i].astype(BF16),
                   w_ple_gate[i].astype(BF16), row(b_ple_gate[i]), w_ple[i].astype(BF16),
                   row(ln_ffn_g[i]), row(ln_ffn_b[i]))
        x2 = _mix_ffn(alpha, x.reshape(b * s, d), att.reshape(b * s, ATT_W), cv,
                      gates.reshape(b * s, 2 * d), p[i].reshape(b * s, PLE_DIM), weights)
        x = x2.reshape(b, s, d)
    return x
```

```python
import functools
import math

import numpy as np
import jax
import jax.numpy as jnp
from jax import lax
from jax.experimental import pallas as pl
from jax.experimental.pallas import tpu as pltpu

D_MODEL = 1024
PLE_DIM = 256
ATT_HEADS = 8
HEAD_DIM = 64
ATT_W = ATT_HEADS * HEAD_DIM
MOBA_BLOCK = 256
MOBA_TOPK = 3
CONV_CH = 512
CONV_WIDTH = 31
NUM_BUCKETS = 32
MAX_DISTANCE = 128
FFN_HIDDEN = 2816
LN_EPS = 1e-5
NEG_INF = -1e30

LANES = 128
SUBLANES = 8
HEADS_PER_STEP = LANES // HEAD_DIM
ONES_ROWS = 16
LOG2E = math.log2(math.e)
HALF = MOBA_BLOCK // 2
ATTN_SLOTS = 4
ATTN_ROWS = 2
PROJ_ROWS = 1024
MIX_ROWS = 512
MIX_ROW_GROUPS = 2
CONV_BLK = LANES
CONV_ROW_TILE = 64
CONV_CH_TILE = 64
CONV_ROW_PITCH = CONV_CH + 8
VMEM_LIMIT = 56 * 1024 * 1024

BF16 = jnp.bfloat16
F32 = jnp.float32
NT_DIMS = (((1,), (1,)), ((), ()))


def _resident(shape):
    return pl.BlockSpec(shape, lambda *_: (0,) * len(shape), pipeline_mode=pl.Buffered(1))


def _layer_norm(h, g, b):
    mu = jnp.mean(h, axis=-1, keepdims=True)
    d = h - mu
    var = jnp.mean(d * d, axis=-1, keepdims=True)
    return d * lax.rsqrt(var + LN_EPS) * g + b


def _t5_bucket_np(rel):
    n = np.maximum(rel, 0)
    max_exact = NUM_BUCKETS // 2
    nf = np.maximum(n, 1).astype(np.float32)
    large = max_exact + (np.log(nf / np.float32(max_exact)) / np.float32(math.log(MAX_DISTANCE / max_exact))
                         * np.float32(NUM_BUCKETS - max_exact)).astype(np.int32)
    large = np.minimum(large, NUM_BUCKETS - 1)
    return np.where(n < max_exact, n, large).astype(np.int32)


def _bias_bucket_tables():
    kpos = np.arange(MOBA_BLOCK)[:, None]
    qpos = np.arange(MOBA_BLOCK)[None, :]
    rel_own = qpos - kpos
    own = np.where(rel_own >= 0, _t5_bucket_np(rel_own), NUM_BUCKETS)
    prev = _t5_bucket_np(rel_own + MOBA_BLOCK)
    return own.astype(np.int32), prev.astype(np.int32)


def _bias_kernel(tbl_ref, own_ix_ref, prev_ix_ref, own_ref, prev_ref):
    h = pl.program_id(0)
    own_ix = own_ix_ref[...]
    prev_ix = prev_ix_ref[...]
    own = jnp.full(own_ix.shape, NEG_INF, F32)
    prev = jnp.zeros(prev_ix.shape, F32)
    for b in range(NUM_BUCKETS):
        t = tbl_ref[h, b] * LOG2E
        own = jnp.where(own_ix == b, t, own)
        prev = jnp.where(prev_ix == b, t, prev)
    own_ref[0] = own
    prev_ref[0] = prev


def _bias_tables(tbl_t):
    own_ix, prev_ix = _bias_bucket_tables()
    blk = (MOBA_BLOCK, MOBA_BLOCK)
    return pl.pallas_call(
        _bias_kernel,
        grid=(ATT_HEADS,),
        in_specs=[pl.BlockSpec(memory_space=pltpu.SMEM),
                  pl.BlockSpec(blk, lambda h: (0, 0)),
                  pl.BlockSpec(blk, lambda h: (0, 0))],
        out_specs=[pl.BlockSpec((1,) + blk, lambda h: (h, 0, 0)),
                   pl.BlockSpec((1,) + blk, lambda h: (h, 0, 0))],
        out_shape=[jax.ShapeDtypeStruct((ATT_HEADS,) + blk, F32)] * 2,
        name="t5_bias",
    )(tbl_t, jnp.asarray(own_ix), jnp.asarray(prev_ix))


def _proj_kernel(x_ref, wqk_ref, wvt_ref, wagt_ref, wgate_ref, bgate_ref,
                 qk_ref, vt_ref, ut_ref, gate_ref):
    xb = x_ref[0].astype(BF16)
    gl = jnp.dot(xb, wgate_ref[...], preferred_element_type=F32) + bgate_ref[...]
    gate_ref[0] = jax.nn.sigmoid(gl).astype(BF16)
    agt = lax.dot_general(wagt_ref[...], xb, NT_DIMS, preferred_element_type=F32)
    ut_ref[0] = (agt[:CONV_CH] * jax.nn.sigmoid(agt[CONV_CH:])).astype(BF16)
    qk = jnp.dot(xb, wqk_ref[...], preferred_element_type=F32)
    qk_ref[0, :, :ATT_W] = (qk[:, :ATT_W] * (LOG2E * HEAD_DIM ** -0.5)).astype(BF16)
    qk_ref[0, :, ATT_W:] = qk[:, ATT_W:].astype(BF16)
    vt = lax.dot_general(wvt_ref[...], xb, NT_DIMS, preferred_element_type=F32)
    vt_ref[0] = vt.astype(BF16)


def _proj(x, wqk, wvt, wagt, wgate, bgate):
    b, s, d = x.shape
    tm = math.gcd(s, PROJ_ROWS)
    return pl.pallas_call(
        _proj_kernel,
        grid=(b, s // tm),
        in_specs=[pl.BlockSpec((1, tm, d), lambda i, j: (i, j, 0)),
                  _resident(wqk.shape), _resident(wvt.shape), _resident(wagt.shape),
                  _resident(wgate.shape), _resident(bgate.shape)],
        out_specs=[pl.BlockSpec((1, tm, 2 * ATT_W), lambda i, j: (i, j, 0)),
                   pl.BlockSpec((1, ATT_W, tm), lambda i, j: (i, 0, j)),
                   pl.BlockSpec((1, CONV_CH, tm), lambda i, j: (i, 0, j)),
                   pl.BlockSpec((1, tm, 2 * d), lambda i, j: (i, j, 0))],
        out_shape=[jax.ShapeDtypeStruct((b, s, 2 * ATT_W), BF16),
                   jax.ShapeDtypeStruct((b, ATT_W, s), BF16),
                   jax.ShapeDtypeStruct((b, CONV_CH, s), BF16),
                   jax.ShapeDtypeStruct((b, s, 2 * d), BF16)],
        compiler_params=pltpu.CompilerParams(
            dimension_semantics=("arbitrary", "arbitrary"), vmem_limit_bytes=VMEM_LIMIT),
        name="in_proj",
    )(x, wqk, wvt, wagt, wgate, bgate)


def _attn_kernel(tbl_ref, q_ref, k_ref, vt_ref, bown_ref, bprev_ref, o_ref,
                 qm_ref, vaug_ref, *sp_refs):
    s_refs, p_refs = sp_refs[:ATTN_SLOTS], sp_refs[ATTN_SLOTS:]
    hp = pl.program_id(0)
    n_rows, s, _ = q_ref.shape
    nb = s // MOBA_BLOCK
    lane = lax.broadcasted_iota(jnp.int32, (1, LANES), 1)
    q_blk = lax.broadcasted_iota(jnp.int32, (1, s), 1) // MOBA_BLOCK
    blk = lax.broadcasted_iota(jnp.int32, (nb, 1), 0)
    past = blk < q_blk
    far_bias = [tbl_ref[hp * HEADS_PER_STEP + hh, NUM_BUCKETS - 1] * LOG2E for hh in range(HEADS_PER_STEP)]

    sel_neg = {}
    for r in range(n_rows):
        q = q_ref[r]
        k = k_ref[r]
        k_mean = (jnp.sum(k.astype(F32).reshape(nb, MOBA_BLOCK, LANES), axis=1)
                  * (1.0 / MOBA_BLOCK)).astype(BF16)
        for hh in range(HEADS_PER_STEP):
            rh = r * HEADS_PER_STEP + hh
            qm = jnp.where(lane // HEAD_DIM == hh, q, jnp.zeros_like(q))
            qm_ref[rh] = qm
            vaug_ref[rh, :HEAD_DIM, :] = vt_ref[r, hh * HEAD_DIM:(hh + 1) * HEAD_DIM, :]
            vaug_ref[rh, HEAD_DIM:, :] = jnp.ones((ONES_ROWS, s), BF16)
            gate = lax.dot_general(k_mean, qm, NT_DIMS, preferred_element_type=F32)
            rank = jnp.zeros((nb, s), jnp.int32)
            for m in range(nb):
                gm = gate[m:m + 1, :]
                beats = (gm > gate) | ((gm == gate) & (m < blk))
                rank = rank + jnp.where(beats & (m < q_blk), 1, 0)
            sel_neg[r, hh] = jnp.where(past & (rank < MOBA_TOPK), 0.0, NEG_INF).astype(F32)

    rows = lambda j: slice(j * MOBA_BLOCK, (j + 1) * MOBA_BLOCK)
    chains = [(i, r, hh) for i in range(nb) for r in range(n_rows) for hh in range(HEADS_PER_STEP)]
    slot_of = {chain: c % ATTN_SLOTS for c, chain in enumerate(chains)}
    shifts = {}
    outs = {}

    def pass_a(i, r, hh):
        hi = (i + 1) * MOBA_BLOCK
        slot = slot_of[i, r, hh]
        st = lax.dot_general(k_ref[r, :hi, :], qm_ref[r * HEADS_PER_STEP + hh, rows(i), :], NT_DIMS,
                             preferred_element_type=F32)
        col_max, offs = None, []
        for j in range(i + 1):
            blk_t = st[rows(j)]
            if j == i:
                top = blk_t[:HALF] + bown_ref[hh, :HALF, :]
                low = blk_t[HALF:, HALF:] + bown_ref[hh, HALF:, HALF:]
                s_refs[slot][j * MOBA_BLOCK:j * MOBA_BLOCK + HALF, :] = top
                s_refs[slot][j * MOBA_BLOCK + HALF:(j + 1) * MOBA_BLOCK, HALF:] = low
                m_top = jnp.max(top, axis=0, keepdims=True)
                m_low = jnp.max(low, axis=0, keepdims=True)
                cand = jnp.concatenate([m_top[:, :HALF], jnp.maximum(m_top[:, HALF:], m_low)], axis=1)
                col_max = cand if col_max is None else jnp.maximum(col_max, cand)
                offs.append(None)
                continue
            elif j == i - 1:
                blk_t = blk_t + bprev_ref[hh]
                off = sel_neg[r, hh][j:j + 1, rows(i)]
            else:
                off = sel_neg[r, hh][j:j + 1, rows(i)] + far_bias[hh]
            s_refs[slot][rows(j), :] = blk_t
            cand = jnp.max(blk_t, axis=0, keepdims=True)
            cand = cand if off is None else cand + off
            col_max = cand if col_max is None else jnp.maximum(col_max, cand)
            offs.append(off)
        shifts[i, r, hh] = [col_max if off is None else col_max - off for off in offs]

    def pass_b(i, r, hh):
        hi = (i + 1) * MOBA_BLOCK
        slot = slot_of[i, r, hh]
        shift = shifts.pop((i, r, hh))
        for j in range(i):
            p_refs[slot][rows(j), :] = jnp.exp2(s_refs[slot][rows(j), :] - shift[j]).astype(BF16)
        top, low = slice(i * MOBA_BLOCK, i * MOBA_BLOCK + HALF), slice(i * MOBA_BLOCK + HALF, (i + 1) * MOBA_BLOCK)
        p_refs[slot][top, :] = jnp.exp2(s_refs[slot][top, :] - shift[i]).astype(BF16)
        p_refs[slot][low, :HALF] = jnp.zeros((HALF, HALF), BF16)
        p_refs[slot][low, HALF:] = jnp.exp2(s_refs[slot][low, HALF:] - shift[i][:, HALF:]).astype(BF16)
        o_aug = jnp.dot(vaug_ref[r * HEADS_PER_STEP + hh, :, :hi], p_refs[slot][:hi, :],
                        preferred_element_type=F32)
        outs[hh] = o_aug[:HEAD_DIM] / o_aug[HEAD_DIM:HEAD_DIM + 1]
        if hh == HEADS_PER_STEP - 1:
            o_pair = [outs.pop(h2) for h2 in range(HEADS_PER_STEP)]
            o_ref[r, rows(i), :] = jnp.concatenate(o_pair, axis=0).T.astype(o_ref.dtype)

    for c in range(len(chains) + ATTN_SLOTS - 1):
        if c < len(chains):
            pass_a(*chains[c])
        if c >= ATTN_SLOTS - 1:
            pass_b(*chains[c - (ATTN_SLOTS - 1)])


def _attention(tbl_t, qk, vt, bias_own, bias_prev):
    b, s, _ = qk.shape
    n_groups = ATT_W // LANES
    n_rows = math.gcd(b, ATTN_ROWS)
    blk = (HEADS_PER_STEP, MOBA_BLOCK, MOBA_BLOCK)
    return pl.pallas_call(
        _attn_kernel,
        grid=(n_groups, b // n_rows),
        in_specs=[pl.BlockSpec(memory_space=pltpu.SMEM),
                  pl.BlockSpec((n_rows, s, LANES), lambda g, i: (i, 0, g)),
                  pl.BlockSpec((n_rows, s, LANES), lambda g, i: (i, 0, n_groups + g)),
                  pl.BlockSpec((n_rows, LANES, s), lambda g, i: (i, g, 0)),
                  pl.BlockSpec(blk, lambda g, i: (g, 0, 0)),
                  pl.BlockSpec(blk, lambda g, i: (g, 0, 0))],
        out_specs=pl.BlockSpec((n_rows, s, LANES), lambda g, i: (i, 0, g)),
        out_shape=jax.ShapeDtypeStruct((b, s, ATT_W), BF16),
        scratch_shapes=[pltpu.VMEM((n_rows * HEADS_PER_STEP, s, LANES), BF16),
                        pltpu.VMEM((n_rows * HEADS_PER_STEP, HEAD_DIM + ONES_ROWS, s), BF16),
                        *[pltpu.VMEM((s, MOBA_BLOCK), F32)] * ATTN_SLOTS,
                        *[pltpu.VMEM((s, MOBA_BLOCK), BF16)] * ATTN_SLOTS],
        compiler_params=pltpu.CompilerParams(
            dimension_semantics=("arbitrary", "arbitrary"), vmem_limit_bytes=VMEM_LIMIT),
        name="moba_attn",
    )(tbl_t, qk, qk, vt, bias_own, bias_prev)


def _conv_kernel(rows_per_seq, cb_ref, a_ref, taps_ref, o_ref, y_ref):
    g = pl.program_id(1)
    nseq, ct, _, _ = a_ref.shape
    rt = nseq * rows_per_seq
    first_blk = lax.broadcasted_iota(jnp.int32, (rt, 1), 0) % rows_per_seq == 0
    for cl in range(ct):
        c = g * ct + cl
        taps = jnp.broadcast_to(taps_ref[cl:cl + 1, :], (CONV_BLK, 2 * CONV_BLK))
        w_c = pltpu.roll(taps, 0, 1, stride=1, stride_axis=0).astype(BF16)
        a_c = a_ref[:, cl].reshape(rt, CONV_BLK)
        res = jnp.dot(a_c, w_c, preferred_element_type=F32)
        carry = pltpu.roll(res[:, CONV_BLK:], 1, 0)
        y = res[:, :CONV_BLK] + jnp.where(first_blk, 0.0, carry) + cb_ref[c]
        for r8 in range(0, rt, SUBLANES):
            y_ref[pl.ds(r8 * CONV_ROW_PITCH + c, SUBLANES, stride=CONV_ROW_PITCH), :] = y[r8:r8 + SUBLANES]

    @pl.when(g == pl.num_programs(1) - 1)
    def _():
        for r in range(rt):
            x = y_ref[r * CONV_ROW_PITCH:r * CONV_ROW_PITCH + CONV_CH, :]
            o_ref[r * CONV_BLK:(r + 1) * CONV_BLK, :] = x.T.astype(o_ref.dtype)


def _conv_module(ut, conv_w, conv_b):
    b, c, s = ut.shape
    rows_per_seq = s // CONV_BLK
    n_rows = b * rows_per_seq
    row_tile = math.gcd(n_rows, CONV_ROW_TILE)
    assert row_tile % rows_per_seq == 0
    a = ut.reshape(b, c, rows_per_seq, CONV_BLK)
    taps = jnp.pad(conv_w[::-1].T.astype(F32), ((0, 0), (0, 2 * CONV_BLK - CONV_WIDTH)))
    return pl.pallas_call(
        functools.partial(_conv_kernel, rows_per_seq),
        grid=(n_rows // row_tile, c // CONV_CH_TILE),
        in_specs=[pl.BlockSpec(memory_space=pltpu.SMEM),
                  pl.BlockSpec((row_tile // rows_per_seq, CONV_CH_TILE, rows_per_seq, CONV_BLK),
                               lambda i, g: (i, g, 0, 0)),
                  pl.BlockSpec((CONV_CH_TILE, 2 * CONV_BLK), lambda i, g: (g, 0))],
        out_specs=pl.BlockSpec((row_tile * CONV_BLK, c), lambda i, g: (i, 0)),
        out_shape=jax.ShapeDtypeStruct((b * s, c), BF16),
        scratch_shapes=[pltpu.VMEM((row_tile * CONV_ROW_PITCH, CONV_BLK), F32)],
        compiler_params=pltpu.CompilerParams(
            dimension_semantics=("arbitrary", "arbitrary"), vmem_limit_bytes=VMEM_LIMIT),
        name="conv_module",
    )(conv_b.astype(F32), a, taps)


def _mix_ffn_kernel(alpha, x_ref, att_ref, cv_ref, gate_ref, p_ref,
                    cg_ref, cb_ref, wa_ref, wc_ref, wm_ref, g1_ref, b1_ref,
                    wg_ref, wu_ref, wd_ref, wpg_ref, bpg_ref, wp_ref, g2_ref, b2_ref, o_ref):
    tm, d = x_ref.shape
    groups = [slice(h * (tm // MIX_ROW_GROUPS), (h + 1) * (tm // MIX_ROW_GROUPS)) for h in range(MIX_ROW_GROUPS)]
    dot = functools.partial(jnp.dot, preferred_element_type=F32)
    y_att = [dot(att_ref[rs, :], wa_ref[...]) for rs in groups]
    cvn = [_layer_norm(cv_ref[rs, :].astype(F32), cg_ref[...], cb_ref[...]) for rs in groups]
    cvn = [(v * jax.nn.sigmoid(v)).astype(BF16) for v in cvn]
    y_conv = [dot(v, wc_ref[...]) for v in cvn]
    merged = [(gate_ref[rs, :d].astype(F32) * ya + gate_ref[rs, d:].astype(F32) * yc).astype(BF16)
              for rs, ya, yc in zip(groups, y_att, y_conv)]
    mixed = [dot(m, wm_ref[...]) for m in merged]
    pe = [dot(p_ref[rs, :].astype(BF16), wp_ref[...]) for rs in groups]
    x1 = [_layer_norm(alpha * x_ref[rs, :] + mx, g1_ref[...], b1_ref[...]) for rs, mx in zip(groups, mixed)]
    xb = [v.astype(BF16) for v in x1]
    hg = [dot(v, wg_ref[...]) for v in xb]
    hu = [dot(v, wu_ref[...]) for v in xb]
    pg = [dot(v, wpg_ref[...]) for v in xb]
    hid = [(g * jax.nn.sigmoid(g) * u).astype(BF16) for g, u in zip(hg, hu)]
    ffn = [dot(v, wd_ref[...]) for v in hid]
    for h, rs in enumerate(groups):
        ple = jax.nn.sigmoid(pg[h] + bpg_ref[...]) * pe[h]
        o_ref[rs, :] = _layer_norm(alpha * x1[h] + ffn[h] + ple, g2_ref[...], b2_ref[...])


def _mix_ffn(alpha, x, att, cv, gates, p, weights):
    t, d = x.shape
    tm = math.gcd(t, MIX_ROWS)
    row = lambda n: pl.BlockSpec((tm, n), lambda i: (i, 0))
    return pl.pallas_call(
        functools.partial(_mix_ffn_kernel, alpha),
        grid=(t // tm,),
        in_specs=[row(d), row(ATT_W), row(CONV_CH), row(2 * d), row(PLE_DIM)]
                 + [_resident(w.shape) for w in weights],
        out_specs=row(d),
        out_shape=jax.ShapeDtypeStruct((t, d), F32),
        compiler_params=pltpu.CompilerParams(
            dimension_semantics=("arbitrary",), vmem_limit_bytes=VMEM_LIMIT),
        name="mix_ffn",
    )(x, att, cv, gates, p, *weights)


def kernel(x, p, w_in, b_gate, bias_table, w_att_out, conv_w, conv_b, conv_ln_g, conv_ln_b,
           w_conv_out, w_mix_out, ln_mix_g, ln_mix_b, w_ffn_gate, w_ffn_up, w_ffn_down,
           w_ple, w_ple_gate, b_ple_gate, ln_ffn_g, ln_ffn_b):
    b, s, d = x.shape
    depth = w_in.shape[0]
    assert d == D_MODEL and s % MOBA_BLOCK == 0
    alpha = (2.0 * depth) ** 0.25
    row = lambda v: v.reshape(1, -1).astype(F32)
    k_end, v_end = 2 * ATT_W, 3 * ATT_W
    u_end = v_end + 2 * CONV_CH

    tbl_t = bias_table.T.astype(F32)
    bias_own, bias_prev = _bias_tables(tbl_t)
    for i in range(depth):
        w = w_in[i].astype(BF16)
        qk, vt, ut, gates = _proj(x, w[:, :k_end], w[:, k_end:v_end].T, w[:, v_end:u_end].T,
                                  w[:, u_end:], row(b_gate[i]))
        att = _attention(tbl_t, qk, vt, bias_own, bias_prev)
        cv = _conv_module(ut, conv_w[i], conv_b[i])
        weights = (row(conv_ln_g[i]), row(conv_ln_b[i]),
                   w_att_out[i].astype(BF16), w_conv_out[i].astype(BF16), w_mix_out[i].astype(BF16),
                   row(ln_mix_g[i]), row(ln_mix_b[i]),
                   w_ffn_gate[i].astype(BF16), w_ffn_up[i].astype(BF16), w_ffn_down[i].astype(BF16),
                   w_ple_gate[i].astype(BF16), row(b_ple_gate[i]), w_ple[i].astype(BF16),
                   row(ln_ffn_g[i]), row(ln_ffn_b[i]))
        x2 = _mix_ffn(alpha, x.reshape(b * s, d), att.reshape(b * s, ATT_W), cv,
                      gates.reshape(b * s, 2 * d), p[i].reshape(b * s, PLE_DIM), weights)
        x = x2.reshape(b, s, d)
    return x
```

```python
import functools
import math

import numpy as np
import jax
import jax.numpy as jnp
from jax import lax
from jax.experimental import pallas as pl
from jax.experimental.pallas import tpu as pltpu

D_MODEL = 1024
PLE_DIM = 256
ATT_HEADS = 8
HEAD_DIM = 64
ATT_W = ATT_HEADS * HEAD_DIM
MOBA_BLOCK = 256
MOBA_TOPK = 3
CONV_CH = 512
CONV_WIDTH = 31
NUM_BUCKETS = 32
MAX_DISTANCE = 128
FFN_HIDDEN = 2816
LN_EPS = 1e-5
NEG_INF = -1e30

LANES = 128
SUBLANES = 8
HEADS_PER_STEP = LANES // HEAD_DIM
ONES_ROWS = 16
LOG2E = math.log2(math.e)
ATTN_SLOTS = 4
ATTN_ROWS = 2
PROJ_ROWS = 1024
MIX_ROWS = 512
MIX_ROW_GROUPS = 2
CONV_BLK = LANES
CONV_ROW_TILE = 64
CONV_CH_TILE = 64
CONV_ROW_PITCH = CONV_CH + 8
VMEM_LIMIT = 56 * 1024 * 1024

BF16 = jnp.bfloat16
F32 = jnp.float32
NT_DIMS = (((1,), (1,)), ((), ()))


def _resident(shape):
    return pl.BlockSpec(shape, lambda *_: (0,) * len(shape), pipeline_mode=pl.Buffered(1))


def _layer_norm(h, g, b):
    mu = jnp.mean(h, axis=-1, keepdims=True)
    d = h - mu
    var = jnp.mean(d * d, axis=-1, keepdims=True)
    return d * lax.rsqrt(var + LN_EPS) * g + b


def _t5_bucket_np(rel):
    n = np.maximum(rel, 0)
    max_exact = NUM_BUCKETS // 2
    nf = np.maximum(n, 1).astype(np.float32)
    large = max_exact + (np.log(nf / np.float32(max_exact)) / np.float32(math.log(MAX_DISTANCE / max_exact))
                         * np.float32(NUM_BUCKETS - max_exact)).astype(np.int32)
    large = np.minimum(large, NUM_BUCKETS - 1)
    return np.where(n < max_exact, n, large).astype(np.int32)


def _bias_bucket_tables():
    kpos = np.arange(MOBA_BLOCK)[:, None]
    qpos = np.arange(MOBA_BLOCK)[None, :]
    rel_own = qpos - kpos
    own = np.where(rel_own >= 0, _t5_bucket_np(rel_own), NUM_BUCKETS)
    prev = _t5_bucket_np(rel_own + MOBA_BLOCK)
    return own.astype(np.int32), prev.astype(np.int32)


def _bias_kernel(tbl_ref, own_ix_ref, prev_ix_ref, own_ref, prev_ref):
    h = pl.program_id(0)
    own_ix = own_ix_ref[...]
    prev_ix = prev_ix_ref[...]
    own = jnp.full(own_ix.shape, NEG_INF, F32)
    prev = jnp.zeros(prev_ix.shape, F32)
    for b in range(NUM_BUCKETS):
        t = tbl_ref[h, b] * LOG2E
        own = jnp.where(own_ix == b, t, own)
        prev = jnp.where(prev_ix == b, t, prev)
    own_ref[0] = own
    prev_ref[0] = prev


def _bias_tables(tbl_t):
    own_ix, prev_ix = _bias_bucket_tables()
    blk = (MOBA_BLOCK, MOBA_BLOCK)
    return pl.pallas_call(
        _bias_kernel,
        grid=(ATT_HEADS,),
        in_specs=[pl.BlockSpec(memory_space=pltpu.SMEM),
                  pl.BlockSpec(blk, lambda h: (0, 0)),
                  pl.BlockSpec(blk, lambda h: (0, 0))],
        out_specs=[pl.BlockSpec((1,) + blk, lambda h: (h, 0, 0)),
                   pl.BlockSpec((1,) + blk, lambda h: (h, 0, 0))],
        out_shape=[jax.ShapeDtypeStruct((ATT_HEADS,) + blk, F32)] * 2,
        name="t5_bias",
    )(tbl_t, jnp.asarray(own_ix), jnp.asarray(prev_ix))


def _proj_kernel(x_ref, wqk_ref, wvt_ref, wagt_ref, wgate_ref, bgate_ref,
                 qk_ref, vt_ref, ut_ref, gate_ref):
    xb = x_ref[0].astype(BF16)
    gl = jnp.dot(xb, wgate_ref[...], preferred_element_type=F32) + bgate_ref[...]
    gate_ref[0] = jax.nn.sigmoid(gl).astype(BF16)
    agt = lax.dot_general(wagt_ref[...], xb, NT_DIMS, preferred_element_type=F32)
    ut_ref[0] = (agt[:CONV_CH] * jax.nn.sigmoid(agt[CONV_CH:])).astype(BF16)
    qk = jnp.dot(xb, wqk_ref[...], preferred_element_type=F32)
    qk_ref[0, :, :ATT_W] = (qk[:, :ATT_W] * (LOG2E * HEAD_DIM ** -0.5)).astype(BF16)
    qk_ref[0, :, ATT_W:] = qk[:, ATT_W:].astype(BF16)
    vt = lax.dot_general(wvt_ref[...], xb, NT_DIMS, preferred_element_type=F32)
    vt_ref[0] = vt.astype(BF16)


def _proj(x, wqk, wvt, wagt, wgate, bgate):
    b, s, d = x.shape
    tm = math.gcd(s, PROJ_ROWS)
    return pl.pallas_call(
        _proj_kernel,
        grid=(b, s // tm),
        in_specs=[pl.BlockSpec((1, tm, d), lambda i, j: (i, j, 0)),
                  _resident(wqk.shape), _resident(wvt.shape), _resident(wagt.shape),
                  _resident(wgate.shape), _resident(bgate.shape)],
        out_specs=[pl.BlockSpec((1, tm, 2 * ATT_W), lambda i, j: (i, j, 0)),
                   pl.BlockSpec((1, ATT_W, tm), lambda i, j: (i, 0, j)),
                   pl.BlockSpec((1, CONV_CH, tm), lambda i, j: (i, 0, j)),
                   pl.BlockSpec((1, tm, 2 * d), lambda i, j: (i, j, 0))],
        out_shape=[jax.ShapeDtypeStruct((b, s, 2 * ATT_W), BF16),
                   jax.ShapeDtypeStruct((b, ATT_W, s), BF16),
                   jax.ShapeDtypeStruct((b, CONV_CH, s), BF16),
                   jax.ShapeDtypeStruct((b, s, 2 * d), BF16)],
        compiler_params=pltpu.CompilerParams(
            dimension_semantics=("arbitrary", "arbitrary"), vmem_limit_bytes=VMEM_LIMIT),
        name="in_proj",
    )(x, wqk, wvt, wagt, wgate, bgate)


def _attn_kernel(tbl_ref, q_ref, k_ref, vt_ref, bown_ref, bprev_ref, o_ref,
                 qm_ref, vaug_ref, *sp_refs):
    s_refs, p_refs = sp_refs[:ATTN_SLOTS], sp_refs[ATTN_SLOTS:]
    hp = pl.program_id(0)
    n_rows, s, _ = q_ref.shape
    nb = s // MOBA_BLOCK
    lane = lax.broadcasted_iota(jnp.int32, (1, LANES), 1)
    q_blk = lax.broadcasted_iota(jnp.int32, (1, s), 1) // MOBA_BLOCK
    blk = lax.broadcasted_iota(jnp.int32, (nb, 1), 0)
    past = blk < q_blk
    far_bias = [tbl_ref[hp * HEADS_PER_STEP + hh, NUM_BUCKETS - 1] * LOG2E for hh in range(HEADS_PER_STEP)]

    sel_neg = {}
    for r in range(n_rows):
        q = q_ref[r]
        k = k_ref[r]
        k_mean = (jnp.sum(k.astype(F32).reshape(nb, MOBA_BLOCK, LANES), axis=1)
                  * (1.0 / MOBA_BLOCK)).astype(BF16)
        for hh in range(HEADS_PER_STEP):
            rh = r * HEADS_PER_STEP + hh
            qm = jnp.where(lane // HEAD_DIM == hh, q, jnp.zeros_like(q))
            qm_ref[rh] = qm
            vaug_ref[rh, :HEAD_DIM, :] = vt_ref[r, hh * HEAD_DIM:(hh + 1) * HEAD_DIM, :]
            vaug_ref[rh, HEAD_DIM:, :] = jnp.ones((ONES_ROWS, s), BF16)
            gate = lax.dot_general(k_mean, qm, NT_DIMS, preferred_element_type=F32)
            rank = jnp.zeros((nb, s), jnp.int32)
            for m in range(nb):
                gm = gate[m:m + 1, :]
                beats = (gm > gate) | ((gm == gate) & (m < blk))
                rank = rank + jnp.where(beats & (m < q_blk), 1, 0)
            sel_neg[r, hh] = jnp.where(past & (rank < MOBA_TOPK), 0.0, NEG_INF).astype(F32)

    rows = lambda j: slice(j * MOBA_BLOCK, (j + 1) * MOBA_BLOCK)
    chains = [(i, r, hh) for i in range(nb) for r in range(n_rows) for hh in range(HEADS_PER_STEP)]
    slot_of = {chain: c % ATTN_SLOTS for c, chain in enumerate(chains)}
    shifts = {}
    outs = {}

    def pass_a(i, r, hh):
        hi = (i + 1) * MOBA_BLOCK
        slot = slot_of[i, r, hh]
        st = lax.dot_general(k_ref[r, :hi, :], qm_ref[r * HEADS_PER_STEP + hh, rows(i), :], NT_DIMS,
                             preferred_element_type=F32)
        col_max, offs = None, []
        for j in range(i + 1):
            blk_t = st[rows(j)]
            if j == i:
                blk_t = blk_t + bown_ref[hh]
                off = None
            elif j == i - 1:
                blk_t = blk_t + bprev_ref[hh]
                off = sel_neg[r, hh][j:j + 1, rows(i)]
            else:
                off = sel_neg[r, hh][j:j + 1, rows(i)] + far_bias[hh]
            s_refs[slot][rows(j), :] = blk_t
            cand = jnp.max(blk_t, axis=0, keepdims=True)
            cand = cand if off is None else cand + off
            col_max = cand if col_max is None else jnp.maximum(col_max, cand)
            offs.append(off)
        shifts[i, r, hh] = [col_max if off is None else col_max - off for off in offs]

    def pass_b(i, r, hh):
        hi = (i + 1) * MOBA_BLOCK
        slot = slot_of[i, r, hh]
        shift = shifts.pop((i, r, hh))
        for j in range(i + 1):
            p_refs[slot][rows(j), :] = jnp.exp2(s_refs[slot][rows(j), :] - shift[j]).astype(BF16)
        o_aug = jnp.dot(vaug_ref[r * HEADS_PER_STEP + hh, :, :hi], p_refs[slot][:hi, :],
                        preferred_element_type=F32)
        outs[hh] = o_aug[:HEAD_DIM] / o_aug[HEAD_DIM:HEAD_DIM + 1]
        if hh == HEADS_PER_STEP - 1:
            o_pair = [outs.pop(h2) for h2 in range(HEADS_PER_STEP)]
            o_ref[r, rows(i), :] = jnp.concatenate(o_pair, axis=0).T.astype(o_ref.dtype)

    for c in range(len(chains) + ATTN_SLOTS - 1):
        if c < len(chains):
            pass_a(*chains[c])
        if c >= ATTN_SLOTS - 1:
            pass_b(*chains[c - (ATTN_SLOTS - 1)])


def _attention(tbl_t, qk, vt, bias_own, bias_prev):
    b, s, _ = qk.shape
    n_groups = ATT_W // LANES
    n_rows = math.gcd(b, ATTN_ROWS)
    blk = (HEADS_PER_STEP, MOBA_BLOCK, MOBA_BLOCK)
    return pl.pallas_call(
        _attn_kernel,
        grid=(n_groups, b // n_rows),
        in_specs=[pl.BlockSpec(memory_space=pltpu.SMEM),
                  pl.BlockSpec((n_rows, s, LANES), lambda g, i: (i, 0, g)),
                  pl.BlockSpec((n_rows, s, LANES), lambda g, i: (i, 0, n_groups + g)),
                  pl.BlockSpec((n_rows, LANES, s), lambda g, i: (i, g, 0)),
                  pl.BlockSpec(blk, lambda g, i: (g, 0, 0)),
                  pl.BlockSpec(blk, lambda g, i: (g, 0, 0))],
        out_specs=pl.BlockSpec((n_rows, s, LANES), lambda g, i: (i, 0, g)),
        out_shape=jax.ShapeDtypeStruct((b, s, ATT_W), BF16),
        scratch_shapes=[pltpu.VMEM((n_rows * HEADS_PER_STEP, s, LANES), BF16),
                        pltpu.VMEM((n_rows * HEADS_PER_STEP, HEAD_DIM + ONES_ROWS, s), BF16),
                        *[pltpu.VMEM((s, MOBA_BLOCK), F32)] * ATTN_SLOTS,
                        *[pltpu.VMEM((s, MOBA_BLOCK), BF16)] * ATTN_SLOTS],
        compiler_params=pltpu.CompilerParams(
            dimension_semantics=("arbitrary", "arbitrary"), vmem_limit_bytes=VMEM_LIMIT),
        name="moba_attn",
    )(tbl_t, qk, qk, vt, bias_own, bias_prev)


def _conv_kernel(rows_per_seq, cb_ref, a_ref, taps_ref, o_ref, y_ref):
    g = pl.program_id(1)
    nseq, ct, _, _ = a_ref.shape
    rt = nseq * rows_per_seq
    first_blk = lax.broadcasted_iota(jnp.int32, (rt, 1), 0) % rows_per_seq == 0
    for cl in range(ct):
        c = g * ct + cl
        taps = jnp.broadcast_to(taps_ref[cl:cl + 1, :], (CONV_BLK, 2 * CONV_BLK))
        w_c = pltpu.roll(taps, 0, 1, stride=1, stride_axis=0).astype(BF16)
        a_c = a_ref[:, cl].reshape(rt, CONV_BLK)
        res = jnp.dot(a_c, w_c, preferred_element_type=F32)
        carry = pltpu.roll(res[:, CONV_BLK:], 1, 0)
        y = res[:, :CONV_BLK] + jnp.where(first_blk, 0.0, carry) + cb_ref[c]
        for r8 in range(0, rt, SUBLANES):
            y_ref[pl.ds(r8 * CONV_ROW_PITCH + c, SUBLANES, stride=CONV_ROW_PITCH), :] = y[r8:r8 + SUBLANES]

    @pl.when(g == pl.num_programs(1) - 1)
    def _():
        for r in range(rt):
            x = y_ref[r * CONV_ROW_PITCH:r * CONV_ROW_PITCH + CONV_CH, :]
            o_ref[r * CONV_BLK:(r + 1) * CONV_BLK, :] = x.T.astype(o_ref.dtype)


def _conv_module(ut, conv_w, conv_b):
    b, c, s = ut.shape
    rows_per_seq = s // CONV_BLK
    n_rows = b * rows_per_seq
    row_tile = math.gcd(n_rows, CONV_ROW_TILE)
    assert row_tile % rows_per_seq == 0
    a = ut.reshape(b, c, rows_per_seq, CONV_BLK)
    taps = jnp.pad(conv_w[::-1].T.astype(F32), ((0, 0), (0, 2 * CONV_BLK - CONV_WIDTH)))
    return pl.pallas_call(
        functools.partial(_conv_kernel, rows_per_seq),
        grid=(n_rows // row_tile, c // CONV_CH_TILE),
        in_specs=[pl.BlockSpec(memory_space=pltpu.SMEM),
                  pl.BlockSpec((row_tile // rows_per_seq, CONV_CH_TILE, rows_per_seq, CONV_BLK),
                               lambda i, g: (i, g, 0, 0)),
                  pl.BlockSpec((CONV_CH_TILE, 2 * CONV_BLK), lambda i, g: (g, 0))],
        out_specs=pl.BlockSpec((row_tile * CONV_BLK, c), lambda i, g: (i, 0)),
        out_shape=jax.ShapeDtypeStruct((b * s, c), BF16),
        scratch_shapes=[pltpu.VMEM((row_tile * CONV_ROW_PITCH, CONV_BLK), F32)],
        compiler_params=pltpu.CompilerParams(
            dimension_semantics=("arbitrary", "arbitrary"), vmem_limit_bytes=VMEM_LIMIT),
        name="conv_module",
    )(conv_b.astype(F32), a, taps)


def _mix_ffn_kernel(alpha, x_ref, att_ref, cv_ref, gate_ref, p_ref,
                    cg_ref, cb_ref, wa_ref, wc_ref, wm_ref, g1_ref, b1_ref,
                    wg_ref, wu_ref, wd_ref, wpg_ref, bpg_ref, wp_ref, g2_ref, b2_ref, o_ref):
    tm, d = x_ref.shape
    groups = [slice(h * (tm // MIX_ROW_GROUPS), (h + 1) * (tm // MIX_ROW_GROUPS)) for h in range(MIX_ROW_GROUPS)]
    dot = functools.partial(jnp.dot, preferred_element_type=F32)
    y_att = [dot(att_ref[rs, :], wa_ref[...]) for rs in groups]
    cvn = [_layer_norm(cv_ref[rs, :].astype(F32), cg_ref[...], cb_ref[...]) for rs in groups]
    cvn = [(v * jax.nn.sigmoid(v)).astype(BF16) for v in cvn]
    y_conv = [dot(v, wc_ref[...]) for v in cvn]
    merged = [(gate_ref[rs, :d].astype(F32) * ya + gate_ref[rs, d:].astype(F32) * yc).astype(BF16)
              for rs, ya, yc in zip(groups, y_att, y_conv)]
    mixed = [dot(m, wm_ref[...]) for m in merged]
    pe = [dot(p_ref[rs, :].astype(BF16), wp_ref[...]) for rs in groups]
    x1 = [_layer_norm(alpha * x_ref[rs, :] + mx, g1_ref[...], b1_ref[...]) for rs, mx in zip(groups, mixed)]
    xb = [v.astype(BF16) for v in x1]
    hg = [dot(v, wg_ref[...]) for v in xb]
    hu = [dot(v, wu_ref[...]) for v in xb]
    pg = [dot(v, wpg_ref[...]) for v in xb]
    hid = [(g * jax.nn.sigmoid(g) * u).astype(BF16) for g, u in zip(hg, hu)]
    ffn = [dot(v, wd_ref[...]) for v in hid]
    for h, rs in enumerate(groups):
        ple = jax.nn.sigmoid(pg[h] + bpg_ref[...]) * pe[h]
        o_ref[rs, :] = _layer_norm(alpha * x1[h] + ffn[h] + ple, g2_ref[...], b2_ref[...])


def _mix_ffn(alpha, x, att, cv, gates, p, weights):
    t, d = x.shape
    tm = math.gcd(t, MIX_ROWS)
    row = lambda n: pl.BlockSpec((tm, n), lambda i: (i, 0))
    return pl.pallas_call(
        functools.partial(_mix_ffn_kernel, alpha),
        grid=(t // tm,),
        in_specs=[row(d), row(ATT_W), row(CONV_CH), row(2 * d), row(PLE_DIM)]
                 + [_resident(w.shape) for w in weights],
        out_specs=row(d),
        out_shape=jax.ShapeDtypeStruct((t, d), F32),
        compiler_params=pltpu.CompilerParams(
            dimension_semantics=("arbitrary",), vmem_limit_bytes=VMEM_LIMIT),
        name="mix_ffn",
    )(x, att, cv, gates, p, *weights)


def kernel(x, p, w_in, b_gate, bias_table, w_att_out, conv_w, conv_b, conv_ln_g, conv_ln_b,
           w_conv_out, w_mix_out, ln_mix_g, ln_mix_b, w_ffn_gate, w_ffn_up, w_ffn_down,
           w_ple, w_ple_gate, b_ple_gate, ln_ffn_g, ln_ffn_b):
    b, s, d = x.shape
    depth = w_in.shape[0]
    assert d == D_MODEL and s % MOBA_BLOCK == 0
    alpha = (2.0 * depth) ** 0.25
    row = lambda v: v.reshape(1, -1).astype(F32)
    k_end, v_end = 2 * ATT_W, 3 * ATT_W
    u_end = v_end + 2 * CONV_CH

    tbl_t = bias_table.T.astype(F32)
    bias_own, bias_prev = _bias_tables(tbl_t)
    for i in range(depth):
        w = w_in[i].astype(BF16)
        qk, vt, ut, gates = _proj(x, w[:, :k_end], w[:, k_end:v_end].T, w[:, v_end:u_end].T,
                                  w[:, u_end:], row(b_gate[i]))
        att = _attention(tbl_t, qk, vt, bias_own, bias_prev)
        cv = _conv_module(ut, conv_w[i], conv_b[i])
        weights = (row(conv_ln_g[i]), row(conv_ln_b[i]),
                   w_att_out[i].astype(BF16), w_conv_out[i].astype(BF16), w_mix_out[i].astype(BF16),
                   row(ln_mix_g[i]), row(ln_mix_b[i]),
                   w_ffn_gate[i].astype(BF16), w_ffn_up[i].astype(BF16), w_ffn_down[i].astype(BF16),
                   w_ple_gate[i].astype(BF16), row(b_ple_gate[i]), w_ple[i].astype(BF16),
                   row(ln_ffn_g[i]), row(ln_ffn_b[i]))
        x2 = _mix_ffn(alpha, x.reshape(b * s, d), att.reshape(b * s, ATT_W), cv,
                      gates.reshape(b * s, 2 * d), p[i].reshape(b * s, PLE_DIM), weights)
        x = x2.reshape(b, s, d)
    return x
```
